```python
import math
import jax
import jax.numpy as jnp
from jax import lax
import numpy as np

D_MODEL = 2048
BATCH = 4
SEQ = 4096
DEPTH = 2
DEC_BATCH = 32
DEC_SEQ = 16
PAST_LEN = 2048

CHUNK = 64
N_EVEN = (DEPTH + 1) // 2
N_ODD = DEPTH // 2

POOL_WINDOWS = (2, 4, 8, 16)
N_POOL_GROUPS = len(POOL_WINDOWS)
POOL_WIDTH = D_MODEL // 2
POOL_GROUP = POOL_WIDTH // N_POOL_GROUPS
POOL_HIST = max(POOL_WINDOWS) - 1
SB_HEADS = 8
SB_HEAD_DIM = 128
SB_WIDTH = SB_HEADS * SB_HEAD_DIM
Q_BLOCK = 128
AB_IN = POOL_WIDTH + 3 * SB_WIDTH
ML_HEADS = 8
ML_QK_DIM = D_MODEL // 16
ML_V_DIM = D_MODEL // 8
ML_QK_WIDTH = ML_HEADS * ML_QK_DIM
ML_V_WIDTH = ML_HEADS * ML_V_DIM
CONV_W = 4
ML_IN = 2 * ML_QK_WIDTH + 2 * ML_V_WIDTH + 2 * ML_HEADS
N_EXPERTS = 32
TOP_K = 4
D_EXPERT = D_MODEL
SWIGLU_LIMIT = 7.0
SWIGLU_ALPHA = 1.702
MOE_BLOCK = 128
DN_ALPHA = (2 * DEPTH) ** 0.25
DN_BETA = (8 * DEPTH) ** -0.25
LN_EPS = 1e-5

kernel_name = 'stream_pool_sb_mlstm_moe'


def layer_norm(x, g, b):
    xf = x.astype(jnp.float32)
    mu = xf.mean(-1, keepdims=True)
    var = jnp.square(xf - mu).mean(-1, keepdims=True)
    return ((xf - mu) * lax.rsqrt(var + LN_EPS) * g + b).astype(x.dtype)


def adaln_terms(c, ada_w, ada_b):
    mod = jax.nn.silu(c) @ ada_w + ada_b
    return jnp.split(mod[:, None, :], 6, axis=-1)


def pool_mix(u, hist, pos0, w_pool, pool_scale):
    B, T, _ = u.shape
    z = jnp.concatenate([hist.astype(u.dtype), u], axis=1)
    zf = z.astype(jnp.float32)
    cs = jnp.concatenate([jnp.zeros_like(zf[:, :1]), jnp.cumsum(zf, axis=1)], axis=1)
    pos = pos0 + jnp.arange(T)
    end = cs[:, POOL_HIST + 1:POOL_HIST + 1 + T]
    means = []
    for g, w in enumerate(POOL_WINDOWS):
        ch = slice(g * POOL_GROUP, (g + 1) * POOL_GROUP)
        start = cs[:, POOL_HIST + 1 - w:POOL_HIST + 1 - w + T, ch]
        count = jnp.minimum(w, pos + 1).astype(jnp.float32)[None, :, None]
        means.append((end[..., ch] - start) / count)
    diff = jnp.concatenate(means, axis=-1) - u.astype(jnp.float32)
    diff = diff.reshape(B, T, N_POOL_GROUPS, POOL_GROUP).astype(u.dtype)
    y = jnp.einsum('btgc,gcd->btgd', diff, w_pool).reshape(B, T, POOL_WIDTH)
    return y * pool_scale, z[:, -POOL_HIST:]


def stick_breaking(q, k, v, q_pos, k_pos):
    z = jnp.einsum('bqhd,bkhd->bhqk', q, k, preferred_element_type=jnp.float32) * SB_HEAD_DIM ** -0.5
    mask = k_pos[None, :] < q_pos[:, None]
    log_keep = jnp.where(mask, jax.nn.log_sigmoid(-z), 0.0)
    after = lax.cumsum(log_keep, axis=3, reverse=True) - log_keep
    a = jnp.where(mask, jnp.exp(jax.nn.log_sigmoid(z) + after), 0.0)
    return jnp.einsum('bhqk,bkhd->bqhd', a.astype(v.dtype), v)


def pool_sb_mixer(h, pool_hist, k_past, v_past, w_in, w_pool, pool_scale, w_out):
    B, T, _ = h.shape
    proj = h @ w_in
    u = proj[..., :POOL_WIDTH]
    q, k, v = (proj[..., POOL_WIDTH + i * SB_WIDTH:POOL_WIDTH + (i + 1) * SB_WIDTH]
               .reshape(B, T, SB_HEADS, SB_HEAD_DIM) for i in range(3))
    past = 0 if k_past is None else k_past.shape[1]
    pool_out, pool_new = pool_mix(u, pool_hist, past, w_pool, pool_scale)
    q_pos = past + jnp.arange(T)
    if k_past is None:
        n_blk = T // Q_BLOCK
        q_blocks = q.reshape(B, n_blk, Q_BLOCK, SB_HEADS, SB_HEAD_DIM).swapaxes(0, 1)
        pos_blocks = q_pos.reshape(n_blk, Q_BLOCK)
        att = lax.map(lambda qp: stick_breaking(qp[0], k, v, qp[1], q_pos), (q_blocks, pos_blocks))
        att = att.swapaxes(0, 1).reshape(B, T, SB_WIDTH)
    else:
        k_all = jnp.concatenate([k_past.astype(k.dtype), k], axis=1)
        v_all = jnp.concatenate([v_past.astype(v.dtype), v], axis=1)
        att = stick_breaking(q, k_all, v_all, q_pos, jnp.arange(past + T)).reshape(B, T, SB_WIDTH)
    y = jnp.concatenate([pool_out, att], axis=-1) @ w_out
    return y, pool_new, k, v


def causal_conv(u, hist, conv_w, conv_b):
    T = u.shape[1]
    z = jnp.concatenate([hist.astype(u.dtype), u], axis=1)
    y = conv_b + z[:, :T] * conv_w[0]
    for j in range(1, CONV_W):
        y = y + z[:, j:j + T] * conv_w[j]
    return y, z[:, -(CONV_W - 1):]


def mlstm_chunk(carry, xs):
    C, n, m = carry
    q, k, v, ig, lf = xs
    L = q.shape[1]
    b = jnp.cumsum(lf, axis=1)
    causal = jnp.tril(jnp.ones((L, L), bool))[None, :, :, None]
    dmat = jnp.where(causal, b[:, :, None, :] - b[:, None, :, :] + ig[:, None, :, :], -jnp.inf)
    inter = b + m[:, None, :]
    m_t = jnp.maximum(inter, dmat.max(axis=2))
    s_ts = jnp.exp(dmat - m_t[:, :, None, :]) * jnp.einsum('bthd,bshd->btsh', q, k)
    decay = jnp.exp(inter - m_t)
    num = jnp.einsum('btsh,bshv->bthv', s_ts, v) + decay[..., None] * jnp.einsum('bthd,bhdv->bthv', q, C)
    den = s_ts.sum(axis=2) + decay * jnp.einsum('bthd,bhd->bth', q, n)
    h = num / jnp.maximum(jnp.abs(den), jnp.exp(-m_t))[..., None]
    b_last = b[:, -1]
    g = b_last[:, None, :] - b + ig
    m_new = jnp.maximum(b_last + m, g.max(axis=1))
    wg = jnp.exp(g - m_new[:, None, :])
    keep = jnp.exp(b_last + m - m_new)
    C_new = keep[..., None, None] * C + jnp.einsum('bsh,bshd,bshv->bhdv', wg, k, v)
    n_new = keep[..., None] * n + jnp.einsum('bsh,bshd->bhd', wg, k)
    return (C_new, n_new, m_new), h


def mlstm_scan(q, k, v, ig, lf, C0, n0, m0, chunk):
    B, T = q.shape[:2]
    nc = T // chunk

    def to_chunks(a):
        return a.reshape((B, nc, chunk) + a.shape[2:]).swapaxes(0, 1)

    (C1, n1, m1), hid = lax.scan(mlstm_chunk, (C0, n0, m0),
                                 (to_chunks(q), to_chunks(k), to_chunks(v), to_chunks(ig), to_chunks(lf)))
    hid = hid.swapaxes(0, 1).reshape((B, T) + hid.shape[3:])
    return hid, C1, n1, m1


def mlstm_mixer(h, conv_hist, C0, n0, m0, w_in, conv_w, conv_b, b_i, b_f, w_out, chunk):
    B, T, _ = h.shape
    proj = h @ w_in
    o1 = 2 * ML_QK_WIDTH
    o2 = o1 + ML_V_WIDTH
    o3 = o2 + ML_V_WIDTH
    o4 = o3 + ML_HEADS
    qk_pre, v, o_pre, i_pre, f_pre = proj[..., :o1], proj[..., o1:o2], proj[..., o2:o3], proj[..., o3:o4], proj[..., o4:]
    qk, conv_new = causal_conv(qk_pre, conv_hist, conv_w, conv_b)
    qk = jax.nn.silu(qk).astype(jnp.float32)
    q = qk[..., :ML_QK_WIDTH].reshape(B, T, ML_HEADS, ML_QK_DIM)
    k = qk[..., ML_QK_WIDTH:].reshape(B, T, ML_HEADS, ML_QK_DIM) * ML_QK_DIM ** -0.5
    vf = v.astype(jnp.float32).reshape(B, T, ML_HEADS, ML_V_DIM)
    ig = (i_pre + b_i).astype(jnp.float32)
    lf = jax.nn.log_sigmoid((f_pre + b_f).astype(jnp.float32))
    hid, C1, n1, m1 = mlstm_scan(q, k, vf, ig, lf, C0, n0, m0, chunk)
    y = (jax.nn.sigmoid(o_pre) * hid.reshape(B, T, ML_V_WIDTH).astype(h.dtype)) @ w_out
    return y, conv_new, C1, n1, m1


def moe_ffn(h, router_w, router_b, w_up, b_up, w_down, b_down):
    B, T, D = h.shape
    n_tok = B * T
    n_pairs = n_tok * TOP_K
    x2 = h.reshape(n_tok, D)
    logits = (x2 @ router_w + router_b).astype(jnp.float32)
    top_logit, top_e = lax.top_k(logits, TOP_K)
    gate = jax.nn.softmax(top_logit, axis=-1)
    flat_e = top_e.reshape(-1)
    flat_tok = jnp.arange(n_pairs) // TOP_K
    order = jnp.argsort(flat_e)
    se, stok, sgate = flat_e[order], flat_tok[order], gate.reshape(-1)[order]
    counts = jnp.bincount(flat_e, length=N_EXPERTS)
    padded = (counts + MOE_BLOCK - 1) // MOE_BLOCK * MOE_BLOCK
    start = jnp.cumsum(counts) - counts
    pend = jnp.cumsum(padded)
    pstart = pend - padded
    dest = pstart[se] + jnp.arange(n_pairs) - start[se]
    n_blocks = -(-n_pairs // MOE_BLOCK) + N_EXPERTS
    row_tok = jnp.zeros(n_blocks * MOE_BLOCK, jnp.int32).at[dest].set(stok.astype(jnp.int32))
    block_e = jnp.minimum(jnp.searchsorted(pend, jnp.arange(n_blocks) * MOE_BLOCK, side='right'), N_EXPERTS - 1)

    def expert_block(args):
        tok, e = args
        hu = x2[tok] @ w_up[e] + b_up[e]
        glu = jnp.minimum(hu[:, :D_EXPERT], SWIGLU_LIMIT)
        lin = jnp.clip(hu[:, D_EXPERT:], -SWIGLU_LIMIT, SWIGLU_LIMIT)
        act = glu * jax.nn.sigmoid(SWIGLU_ALPHA * glu) * (lin + 1)
        return act @ w_down[e] + b_down[e]

    rows = lax.map(expert_block, (row_tok.reshape(n_blocks, MOE_BLOCK), block_e)).reshape(-1, D)
    contrib = rows[dest] * sgate[:, None].astype(rows.dtype)
    return jnp.zeros_like(x2).at[stok].add(contrib).reshape(B, T, D)


def run_trunk(x, c, states, ab, ml, shared):
    pool_hist, k_past, v_past, conv_hist, st_C, st_n, st_m = states
    ab_w_in, ab_w_pool, ab_pool_scale, ab_w_out = ab
    ml_w_in, ml_conv_w, ml_conv_b, ml_b_i, ml_b_f, ml_w_out = ml
    ada_w, ada_b, ln_g, ln_b, router_w, router_b, w_up, b_up, w_down, b_down = shared
    first_chunk = k_past is None
    B, T, _ = x.shape
    pools, ks, vs, convs, Cs, ns, ms = [], [], [], [], [], [], []
    for layer in range(DEPTH):
        sh_m, sc_m, g_m, sh_f, sc_f, g_f = adaln_terms(c, ada_w[layer], ada_b[layer])
        h = x * (1 + sc_m) + sh_m
        j = layer // 2
        if layer % 2 == 0:
            if first_chunk:
                ph, kp, vp = jnp.zeros((B, POOL_HIST, POOL_WIDTH), x.dtype), None, None
            else:
                ph, kp, vp = pool_hist[j], k_past[j], v_past[j]
            y, p_new, k_new, v_new = pool_sb_mixer(h, ph, kp, vp, ab_w_in[j], ab_w_pool[j],
                                                   ab_pool_scale[j], ab_w_out[j])
            pools.append(p_new)
            ks.append(k_new)
            vs.append(v_new)
        else:
            if first_chunk:
                ch = jnp.zeros((B, CONV_W - 1, 2 * ML_QK_WIDTH), x.dtype)
                C0 = jnp.zeros((B, ML_HEADS, ML_QK_DIM, ML_V_DIM), jnp.float32)
                n0 = jnp.zeros((B, ML_HEADS, ML_QK_DIM), jnp.float32)
                m0 = jnp.zeros((B, ML_HEADS), jnp.float32)
                chunk = CHUNK
            else:
                ch = conv_hist[j]
                C0 = st_C[j].astype(jnp.float32)
                n0 = st_n[j].astype(jnp.float32)
                m0 = st_m[j].astype(jnp.float32)
                chunk = T
            y, c_new, C1, n1, m1 = mlstm_mixer(h, ch, C0, n0, m0, ml_w_in[j], ml_conv_w[j], ml_conv_b[j],
                                               ml_b_i[j], ml_b_f[j], ml_w_out[j], chunk)
            convs.append(c_new)
            Cs.append(C1)
            ns.append(n1)
            ms.append(m1)
        x = layer_norm(DN_ALPHA * x + (1 + g_m) * y, ln_g[layer, 0], ln_b[layer, 0])
        h = x * (1 + sc_f) + sh_f
        y = moe_ffn(h, router_w[layer], router_b[layer], w_up[layer], b_up[layer], w_down[layer], b_down[layer])
        x = layer_norm(DN_ALPHA * x + (1 + g_f) * y, ln_g[layer, 1], ln_b[layer, 1])
    return (x, jnp.stack(pools), jnp.stack(ks), jnp.stack(vs), jnp.stack(convs),
            jnp.stack(Cs), jnp.stack(ns), jnp.stack(ms))


def setup_inputs(seed: int = 0) -> dict:
    key = jax.random.key(seed)
    keys = iter(jax.random.split(key, 40))

    def nrm(shape, std=1.0):
        return std * jax.random.normal(next(keys), shape, jnp.float32)

    D = D_MODEL
    return {
        'x_prompt': nrm((BATCH, SEQ, D)),
        'x_sample': nrm((DEC_BATCH, DEC_SEQ, D)),
        'c_prompt': nrm((BATCH, D)),
        'c_sample': nrm((DEC_BATCH, D)),
        'cache_pool': nrm((N_EVEN, DEC_BATCH, POOL_HIST, POOL_WIDTH)),
        'cache_k': nrm((N_EVEN, DEC_BATCH, PAST_LEN, SB_HEADS, SB_HEAD_DIM)),
        'cache_v': nrm((N_EVEN, DEC_BATCH, PAST_LEN, SB_HEADS, SB_HEAD_DIM)),
        'cache_conv': nrm((N_ODD, DEC_BATCH, CONV_W - 1, 2 * ML_QK_WIDTH)),
        'state_C': nrm((N_ODD, DEC_BATCH, ML_HEADS, ML_QK_DIM, ML_V_DIM), 0.1),
        'state_n': nrm((N_ODD, DEC_BATCH, ML_HEADS, ML_QK_DIM), 0.1),
        'state_m': nrm((N_ODD, DEC_BATCH, ML_HEADS)),
        'ab_w_in': nrm((N_EVEN, D, AB_IN), D ** -0.5),
        'ab_w_pool': nrm((N_EVEN, N_POOL_GROUPS, POOL_GROUP, POOL_GROUP), POOL_GROUP ** -0.5),
        'ab_pool_scale': 1.0 + nrm((N_EVEN, POOL_WIDTH), 0.1),
        'ab_w_out': nrm((N_EVEN, POOL_WIDTH + SB_WIDTH, D), DN_BETA * (POOL_WIDTH + SB_WIDTH) ** -0.5),
        'ml_w_in': nrm((N_ODD, D, ML_IN), D ** -0.5),
        'ml_conv_w': nrm((N_ODD, CONV_W, 2 * ML_QK_WIDTH), CONV_W ** -0.5),
        'ml_conv_b': nrm((N_ODD, 2 * ML_QK_WIDTH), 0.01),
        'ml_b_i': nrm((N_ODD, ML_HEADS), 0.1),
        'ml_b_f': 3.0 + 3.0 * jax.random.uniform(next(keys), (N_ODD, ML_HEADS), jnp.float32),
        'ml_w_out': nrm((N_ODD, ML_V_WIDTH, D), DN_BETA * ML_V_WIDTH ** -0.5),
        'ada_w': nrm((DEPTH, D, 6 * D), 0.5 * D ** -0.5),
        'ada_b': nrm((DEPTH, 6 * D), 0.01),
        'ln_g': 1.0 + nrm((DEPTH, 2, D), 0.05),
        'ln_b': nrm((DEPTH, 2, D), 0.01),
        'router_w': nrm((DEPTH, D, N_EXPERTS), D ** -0.5),
        'router_b': nrm((DEPTH, N_EXPERTS), 0.01),
        'moe_w_up': nrm((DEPTH, N_EXPERTS, D, 2 * D_EXPERT), D ** -0.5),
        'moe_b_up': nrm((DEPTH, N_EXPERTS, 2 * D_EXPERT), 0.01),
        'moe_w_down': nrm((DEPTH, N_EXPERTS, D_EXPERT, D), DN_BETA * D_EXPERT ** -0.5),
        'moe_b_down': nrm((DEPTH, N_EXPERTS, D), 0.01),
    }


def reference(x_prompt, x_sample, c_prompt, c_sample, cache_pool, cache_k, cache_v, cache_conv,
              state_C, state_n, state_m, ab_w_in, ab_w_pool, ab_pool_scale, ab_w_out,
              ml_w_in, ml_conv_w, ml_conv_b, ml_b_i, ml_b_f, ml_w_out, ada_w, ada_b, ln_g, ln_b,
              router_w, router_b, moe_w_up, moe_b_up, moe_w_down, moe_b_down):
    ab = (ab_w_in, ab_w_pool, ab_pool_scale, ab_w_out)
    ml = (ml_w_in, ml_conv_w, ml_conv_b, ml_b_i, ml_b_f, ml_w_out)
    shared = (ada_w, ada_b, ln_g, ln_b, router_w, router_b, moe_w_up, moe_b_up, moe_w_down, moe_b_down)
    (y_prompt, prompt_pool, prompt_k, prompt_v, prompt_conv, prompt_C, prompt_n, prompt_m) = run_trunk(
        x_prompt, c_prompt, (None, None, None, None, None, None, None), ab, ml, shared)
    (y_sample, sample_pool, sample_k, sample_v, sample_conv, sample_C, sample_n, sample_m) = run_trunk(
        x_sample, c_sample, (cache_pool, cache_k, cache_v, cache_conv, state_C, state_n, state_m), ab, ml, shared)
    return (y_prompt, y_sample, prompt_pool, prompt_k, prompt_v, prompt_conv, prompt_C, prompt_n, prompt_m,
            sample_pool, sample_k, sample_v, sample_conv, sample_C, sample_n, sample_m)
```

```python
import functools

import jax
import jax.numpy as jnp
from jax import lax
from jax.experimental import pallas as pl
from jax.experimental.pallas import tpu as pltpu

F32 = jnp.float32
BF16 = jnp.bfloat16
I32 = jnp.int32

TOP_K = 4
POOL_WINDOWS = (2, 4, 8, 16)
POOL_HALO = 16
CONV_W = 4
CONV_HALO = 8
SWIGLU_LIMIT = 7.0
SWIGLU_ALPHA = 1.702
LN_EPS = 1e-5
LANES = 128
MOE_ROWS = 256
GATHER_ROWS = 128
COMBINE_TOKENS = 64
VMEM_LIMIT = 56 * 1024 * 1024
NEG_BIG = -1e30

_HIGHEST = lax.Precision.HIGHEST


def _params(sem, vmem=None):
    return pltpu.CompilerParams(dimension_semantics=sem, vmem_limit_bytes=vmem or VMEM_LIMIT)


def _log_sigmoid(z):
    return jnp.minimum(z, 0.0) - jnp.log(1.0 + jnp.exp(-jnp.abs(z)))


def _adaln_kernel(c_ref, w_ref, b_ref, o_ref):
    c = c_ref[...]
    s = (c * jax.nn.sigmoid(c)).astype(BF16)
    o_ref[...] = jnp.dot(s, w_ref[...].astype(BF16), preferred_element_type=F32) + b_ref[...]


def adaln(c_all, ada_w, ada_b):
    nl, d, n = ada_w.shape
    c = c_all.shape[0]
    tn = min(1024, n)
    return pl.pallas_call(
        _adaln_kernel,
        grid=(nl, n // tn),
        in_specs=[pl.BlockSpec((c, d), lambda l, j: (0, 0)),
                  pl.BlockSpec((None, d, tn), lambda l, j: (l, 0, j)),
                  pl.BlockSpec((None, 1, tn), lambda l, j: (l, 0, j))],
        out_specs=pl.BlockSpec((None, c, tn), lambda l, j: (l, 0, j)),
        out_shape=jax.ShapeDtypeStruct((nl, c, n), F32),
        compiler_params=_params(("arbitrary", "arbitrary")),
        name="adaln",
    )(c_all, ada_w, ada_b.reshape(nl, 1, n))


def _modulate_kernel(x_ref, sc_ref, sh_ref, o_ref):
    o_ref[...] = (x_ref[...] * (1.0 + sc_ref[...]) + sh_ref[...]).astype(o_ref.dtype)


def modulate(x, sc, sh):
    g, r, d = x.shape
    tr = min(512, r)
    out = pl.pallas_call(
        _modulate_kernel,
        grid=(g, r // tr),
        in_specs=[pl.BlockSpec((None, tr, d), lambda b, i: (b, i, 0)),
                  pl.BlockSpec((None, 1, d), lambda b, i: (b, 0, 0)),
                  pl.BlockSpec((None, 1, d), lambda b, i: (b, 0, 0))],
        out_specs=pl.BlockSpec((None, tr, d), lambda b, i: (b, i, 0)),
        out_shape=jax.ShapeDtypeStruct((g, r, d), BF16),
        compiler_params=_params(("arbitrary", "arbitrary")),
        name="modulate",
    )(x, sc, sh)
    return out.reshape(g * r, d)


def _matmul_kernel(*refs, n_a):
    a_refs, w_refs, o_ref = refs[:n_a], refs[n_a:2 * n_a], refs[2 * n_a]
    acc = jnp.dot(a_refs[0][...], w_refs[0][...], preferred_element_type=F32)
    for a, w in zip(a_refs[1:], w_refs[1:]):
        acc = acc + jnp.dot(a[...], w[...], preferred_element_type=F32)
    o_ref[...] = acc.astype(o_ref.dtype)


def matmul(a_list, w, row_offs, col_off, n_cols, out_dtype=F32, tm=512, tn=1024):
    m = a_list[0].shape[0]
    tm = min(tm, m)
    tn = min(tn, n_cols)
    assert m % tm == 0 and n_cols % tn == 0 and col_off % tn == 0
    n_a = len(a_list)
    in_specs = [pl.BlockSpec((tm, a.shape[1]), lambda j, i: (i, 0)) for a in a_list]
    for a, ro in zip(a_list, row_offs):
        ka = a.shape[1]
        assert ro % ka == 0
        in_specs.append(pl.BlockSpec((ka, tn), functools.partial(
            lambda j, i, rb, cb: (rb, cb + j), rb=ro // ka, cb=col_off // tn)))
    return pl.pallas_call(
        functools.partial(_matmul_kernel, n_a=n_a),
        grid=(n_cols // tn, m // tm),
        in_specs=in_specs,
        out_specs=pl.BlockSpec((tm, tn), lambda j, i: (i, j)),
        out_shape=jax.ShapeDtypeStruct((m, n_cols), out_dtype),
        compiler_params=_params(("arbitrary", "arbitrary")),
        name="matmul",
    )(*a_list, *([w] * n_a))


def _pool_kernel(u_ref, hist_ref, w_ref, scale_ref, o_ref, z_ref, *, tt, pg, pos0):
    ti = pl.program_id(1)

    @pl.when(ti == 0)
    def _():
        z_ref[0:POOL_HALO, :] = hist_ref[...]

    @pl.when(ti > 0)
    def _():
        z_ref[0:POOL_HALO, :] = z_ref[tt:tt + POOL_HALO, :]

    z_ref[POOL_HALO:POOL_HALO + tt, :] = u_ref[...]
    pos = pos0 + ti * tt + lax.broadcasted_iota(I32, (tt, 1), 0)
    for g, w in enumerate(POOL_WINDOWS):
        cols = slice(g * pg, (g + 1) * pg)
        u_g = z_ref[POOL_HALO:POOL_HALO + tt, cols]
        acc = u_g
        for j in range(1, w):
            acc = acc + z_ref[POOL_HALO - j:POOL_HALO - j + tt, cols]
        cnt = jnp.minimum(w, pos + 1).astype(F32)
        diff = (acc / cnt - u_g).astype(BF16)
        y = jnp.dot(diff, w_ref[g], preferred_element_type=F32) * scale_ref[:, cols]
        o_ref[:, cols] = y.astype(o_ref.dtype)


def pool_mix(u, hist, w_pool, scale, pos0):
    b, t, pw = u.shape
    ng, pg, _ = w_pool.shape
    tt = min(512, t)
    assert tt >= POOL_HALO and t % tt == 0 and ng == len(POOL_WINDOWS)
    out = pl.pallas_call(
        functools.partial(_pool_kernel, tt=tt, pg=pg, pos0=pos0),
        grid=(b, t // tt),
        in_specs=[pl.BlockSpec((None, tt, pw), lambda i, j: (i, j, 0)),
                  pl.BlockSpec((None, POOL_HALO, pw), lambda i, j: (i, 0, 0)),
                  pl.BlockSpec((ng, pg, pg), lambda i, j: (0, 0, 0)),
                  pl.BlockSpec((1, pw), lambda i, j: (0, 0))],
        out_specs=pl.BlockSpec((None, tt, pw), lambda i, j: (i, j, 0)),
        out_shape=jax.ShapeDtypeStruct((b, t, pw), BF16),
        scratch_shapes=[pltpu.VMEM((tt + POOL_HALO, pw), F32)],
        compiler_params=_params(("arbitrary", "arbitrary")),
        name="pool_mix",
    )(u, hist, w_pool, scale)
    return out.reshape(b * t, pw)


def _sb_block(z, lsz_mask, tri, v_bf, r_prev):
    lsz = _log_sigmoid(z)
    lk = lsz - z
    if lsz_mask is not None:
        lk = jnp.where(lsz_mask, lk, 0.0)
    hi = lk.astype(BF16)
    lo = (lk - hi.astype(F32)).astype(BF16)
    after = (jnp.dot(hi, tri, preferred_element_type=F32)
             + jnp.dot(lo, tri, preferred_element_type=F32))
    a = jnp.exp(lsz + after + r_prev)
    if lsz_mask is not None:
        a = jnp.where(lsz_mask, a, 0.0)
    pv = jnp.dot(a.astype(BF16), v_bf, preferred_element_type=F32)
    return pv, jnp.sum(lk, axis=1, keepdims=True)


def _attn_prompt_kernel(q_ref, k_ref, v_ref, tri_ref, o_ref, r_ref, acc_ref, *, scale, tq):
    qi = pl.program_id(2)
    j = pl.program_id(3)

    @pl.when(j == 0)
    def _():
        r_ref[...] = jnp.zeros_like(r_ref)
        acc_ref[...] = jnp.zeros_like(acc_ref)

    def step(masked):
        q = q_ref[...].astype(BF16)
        k = k_ref[...].astype(BF16)
        z = lax.dot_general(q, k, (((1,), (1,)), ((), ())), preferred_element_type=F32) * scale
        mask = None
        if masked:
            row = lax.broadcasted_iota(I32, (tq, tq), 0)
            col = lax.broadcasted_iota(I32, (tq, tq), 1)
            mask = col < row
        pv, lk_sum = _sb_block(z, mask, tri_ref[...], v_ref[...].astype(BF16), r_ref[...])
        acc_ref[...] += pv
        r_ref[...] += lk_sum

    pl.when(j == 0)(lambda: step(True))
    pl.when(jnp.logical_and(j > 0, j <= qi))(lambda: step(False))

    @pl.when(j == pl.num_programs(3) - 1)
    def _():
        o_ref[...] = acc_ref[...].astype(o_ref.dtype)


def attn_prompt(q, k, v, tri, heads, tq):
    b, t, w = q.shape
    d = w // heads
    nq = t // tq
    kv_spec = pl.BlockSpec((None, tq, d), lambda bi, h, qi, j: (bi, jnp.maximum(qi - j, 0), h))
    out = pl.pallas_call(
        functools.partial(_attn_prompt_kernel, scale=d ** -0.5, tq=tq),
        grid=(b, heads, nq, nq),
        in_specs=[pl.BlockSpec((None, tq, d), lambda bi, h, qi, j: (bi, qi, h)),
                  kv_spec, kv_spec,
                  pl.BlockSpec((tq, tq), lambda bi, h, qi, j: (0, 0))],
        out_specs=pl.BlockSpec((None, tq, d), lambda bi, h, qi, j: (bi, qi, h)),
        out_shape=jax.ShapeDtypeStruct((b, t, w), BF16),
        scratch_shapes=[pltpu.VMEM((tq, 1), F32), pltpu.VMEM((tq, d), F32)],
        compiler_params=_params(("arbitrary",) * 4),
        name="attn_prompt",
    )(q, k, v, tri)
    return out.reshape(b * t, w)


def _attn_sample_kernel(q_ref, kn_ref, vn_ref, kp_ref, vp_ref, tri_ref, o_ref, r_ref, *, scale, ts, tkn, tkp):
    j = pl.program_id(1)
    rows = q_ref.shape[0]

    def block(k_ref, v_ref, tk, masked):
        z = lax.dot_general(q_ref[...], k_ref[...].astype(BF16), (((1,), (1,)), ((), ())),
                            preferred_element_type=F32) * scale
        mask = None
        if masked:
            row = lax.rem(lax.broadcasted_iota(I32, (rows, tk), 0), ts)
            col = lax.broadcasted_iota(I32, (rows, tk), 1)
            mask = col < row
        pv, lk_sum = _sb_block(z, mask, tri_ref[0:tk, 0:tk], v_ref[...].astype(BF16), r_ref[...])
        o_ref[...] += pv
        r_ref[...] += lk_sum

    @pl.when(j == 0)
    def _():
        r_ref[...] = jnp.zeros_like(r_ref)
        o_ref[...] = jnp.zeros_like(o_ref)
        block(kn_ref, vn_ref, tkn, True)

    @pl.when(j > 0)
    def _():
        block(kp_ref, vp_ref, tkp, False)


def attn_sample(q_bd, k_new, v_new, k_past, v_past, tri, ts, d):
    b, rows, w = q_bd.shape
    tkn = k_new.shape[1]
    past = k_past.shape[1]
    tkp = min(512, past)
    assert past % tkp == 0 and tri.shape[0] >= max(tkn, tkp)
    n_past = past // tkp
    past_spec = pl.BlockSpec((None, tkp, w), lambda bi, j: (bi, n_past - jnp.maximum(j, 1), 0))
    new_spec = pl.BlockSpec((None, tkn, w), lambda bi, j: (bi, 0, 0))
    return pl.pallas_call(
        functools.partial(_attn_sample_kernel, scale=d ** -0.5, ts=ts, tkn=tkn, tkp=tkp),
        grid=(b, 1 + n_past),
        in_specs=[pl.BlockSpec((None, rows, w), lambda bi, j: (bi, 0, 0)),
                  new_spec, new_spec, past_spec, past_spec,
                  pl.BlockSpec(tri.shape, lambda bi, j: (0, 0))],
        out_specs=pl.BlockSpec((None, rows, w), lambda bi, j: (bi, 0, 0)),
        out_shape=jax.ShapeDtypeStruct((b, rows, w), F32),
        scratch_shapes=[pltpu.VMEM((rows, 1), F32)],
        compiler_params=_params(("arbitrary", "arbitrary")),
        name="attn_sample",
    )(q_bd, k_new, v_new, k_past, v_past, tri)


def _mlstm_kernel(qk_ref, v_ref, op_ref, g_ref, cw_ref, cb_ref, gb_ref, hist_ref, c0_ref, n0_ref, m0_ref,
                  tril_ref, triu_ref,
                  hid_ref, c1_ref, n1_ref, m1_ref,
                  z_ref, c_ref, n_ref, m_ref, *, L, H, Dk, Dv, n_valid, kscale):
    c = pl.program_id(1)
    qw = H * Dk

    @pl.when(c == 0)
    def _():
        z_ref[0:CONV_HALO, :] = hist_ref[...]
        c_ref[...] = c0_ref[...]
        n_ref[...] = n0_ref[...]
        m_ref[...] = m0_ref[...]

    @pl.when(c > 0)
    def _():
        z_ref[0:CONV_HALO, :] = z_ref[L:L + CONV_HALO, :]

    z_ref[CONV_HALO:CONV_HALO + L, :] = qk_ref[...]
    y = cb_ref[...]
    for j in range(CONV_W):
        r0 = CONV_HALO - (CONV_W - 1) + j
        y = y + z_ref[r0:r0 + L, :] * cw_ref[j:j + 1, :]
    qk = y * jax.nn.sigmoid(y)

    g = g_ref[...] + gb_ref[...]
    lane = lax.broadcasted_iota(I32, (L, LANES), 1)
    gl = jnp.where(jnp.logical_and(lane >= H, lane < 2 * H), _log_sigmoid(g), g)
    if n_valid < L:
        row = lax.broadcasted_iota(I32, (L, LANES), 0)
        gl = jnp.where(row < n_valid, gl, jnp.where(lane < H, NEG_BIG, 0.0))
    glt = gl.T
    b_col = jnp.dot(tril_ref[...], gl, preferred_element_type=F32, precision=_HIGHEST)
    b_row = jnp.dot(glt, triu_ref[...], preferred_element_type=F32, precision=_HIGHEST)
    trow = lax.broadcasted_iota(I32, (L, L), 0)
    tcol = lax.broadcasted_iota(I32, (L, L), 1)
    causal = tcol <= trow

    for h in range(H):
        m_prev = m_ref[:, h:h + 1]
        bc = b_col[:, H + h:H + h + 1]
        br = b_row[H + h:H + h + 1, :]
        igc = gl[:, h:h + 1]
        igr = glt[h:h + 1, :]
        dm = jnp.where(causal, bc - br + igr, -jnp.inf)
        inter = bc + m_prev
        m_t = jnp.maximum(inter, jnp.max(dm, axis=1, keepdims=True))
        qh = qk[:, h * Dk:(h + 1) * Dk]
        kh = qk[:, qw + h * Dk:qw + (h + 1) * Dk] * kscale
        qb = qh.astype(BF16)
        vb = v_ref[:, h * Dv:(h + 1) * Dv].astype(BF16)
        s = jnp.exp(dm - m_t) * lax.dot_general(qb, kh.astype(BF16), (((1,), (1,)), ((), ())),
                                                 preferred_element_type=F32)
        decay = jnp.exp(inter - m_t)
        ch = c_ref[h]
        nh = n_ref[h:h + 1, :]
        num = (jnp.dot(s.astype(BF16), vb, preferred_element_type=F32)
               + decay * jnp.dot(qb, ch.astype(BF16), preferred_element_type=F32))
        den = jnp.sum(s, axis=1, keepdims=True) + decay * jnp.sum(qh * nh, axis=1, keepdims=True)
        hval = num / jnp.maximum(jnp.abs(den), jnp.exp(-m_t))
        og = jax.nn.sigmoid(op_ref[:, h * Dv:(h + 1) * Dv])
        hid_ref[:, h * Dv:(h + 1) * Dv] = (og * hval).astype(hid_ref.dtype)

        b_last = bc[L - 1:L, :]
        gc = b_last - bc + igc
        m_new = jnp.maximum(b_last + m_prev, jnp.max(gc, axis=0, keepdims=True))
        kw = kh * jnp.exp(gc - m_new)
        keep = jnp.exp(b_last + m_prev - m_new)
        c_ref[h] = keep * ch + jnp.dot(kw.T.astype(BF16), vb, preferred_element_type=F32)
        n_ref[h:h + 1, :] = keep * nh + jnp.sum(kw, axis=0, keepdims=True)
        m_ref[:, h:h + 1] = m_new

    @pl.when(c == pl.num_programs(1) - 1)
    def _():
        c1_ref[...] = c_ref[...]
        n1_ref[...] = n_ref[...]
        m1_ref[...] = m_ref[...]


def mlstm_mix(qk_pre, v, o_pre, gates, conv_w, conv_b, gate_b, hist, c0, n0, m0, L, n_valid):
    b, t, qk2 = qk_pre.shape
    _, hh, dk, dv = c0.shape
    vw = hh * dv
    assert t % L == 0
    tril = jnp.tril(jnp.ones((L, L), F32))
    row3 = lambda i, j: (i, j, 0)
    fix3 = lambda i, j: (i, 0, 0)
    fix2 = lambda i, j: (0, 0)
    return pl.pallas_call(
        functools.partial(_mlstm_kernel, L=L, H=hh, Dk=dk, Dv=dv, n_valid=n_valid, kscale=dk ** -0.5),
        grid=(b, t // L),
        in_specs=[pl.BlockSpec((None, L, qk2), row3),
                  pl.BlockSpec((None, L, vw), row3),
                  pl.BlockSpec((None, L, vw), row3),
                  pl.BlockSpec((None, L, LANES), row3),
                  pl.BlockSpec(conv_w.shape, fix2),
                  pl.BlockSpec(conv_b.shape, fix2),
                  pl.BlockSpec(gate_b.shape, fix2),
                  pl.BlockSpec((None, CONV_HALO, qk2), fix3),
                  pl.BlockSpec((None, hh, dk, dv), lambda i, j: (i, 0, 0, 0)),
                  pl.BlockSpec((None, hh, dk), fix3),
                  pl.BlockSpec((None, 1, LANES), fix3),
                  pl.BlockSpec((L, L), fix2),
                  pl.BlockSpec((L, L), fix2)],
        out_specs=[pl.BlockSpec((None, L, vw), row3),
                   pl.BlockSpec((None, hh, dk, dv), lambda i, j: (i, 0, 0, 0)),
                   pl.BlockSpec((None, hh, dk), fix3),
                   pl.BlockSpec((None, 1, LANES), fix3)],
        out_shape=[jax.ShapeDtypeStruct((b, t, vw), BF16),
                   jax.ShapeDtypeStruct(c0.shape, F32),
                   jax.ShapeDtypeStruct(n0.shape, F32),
                   jax.ShapeDtypeStruct(m0.shape, F32)],
        scratch_shapes=[pltpu.VMEM((L + CONV_HALO, qk2), F32),
                        pltpu.VMEM((hh, dk, dv), F32),
                        pltpu.VMEM((hh, dk), F32),
                        pltpu.VMEM((1, LANES), F32)],
        compiler_params=_params(("arbitrary", "arbitrary")),
        name="mlstm_mix",
    )(qk_pre, v, o_pre, gates, conv_w, conv_b, gate_b, hist, c0, n0, m0, tril, tril.T)


def _ln_kernel(*refs, alpha, with_next, with_router, n_experts, next_dtype):
    x_ref, y_ref, gate_ref, lg_ref, lb_ref = refs[:5]
    pos = 5
    if with_next:
        sc_ref, sh_ref = refs[pos:pos + 2]
        pos += 2
    if with_router:
        rw_ref, rb_ref = refs[pos:pos + 2]
        pos += 2
    n_out = 1 + int(with_next) + 2 * int(with_router)
    outs = refs[len(refs) - n_out:]
    xo_ref = outs[0]

    v = alpha * x_ref[...] + (1.0 + gate_ref[...]) * y_ref[...]
    mu = jnp.mean(v, axis=-1, keepdims=True)
    vc = v - mu
    var = jnp.mean(vc * vc, axis=-1, keepdims=True)
    xn = vc * lax.rsqrt(var + LN_EPS) * lg_ref[...] + lb_ref[...]
    xo_ref[...] = xn
    if not with_next:
        return
    h = xn * (1.0 + sc_ref[...]) + sh_ref[...]
    outs[1][...] = h.astype(next_dtype)
    if not with_router:
        return
    tm = h.shape[0]
    logits = jnp.dot(h.astype(BF16), rw_ref[...].astype(BF16), preferred_element_type=F32) + rb_ref[...]
    lane = lax.broadcasted_iota(I32, (tm, LANES), 1).astype(F32)
    logits = jnp.where(lane < n_experts, logits, -jnp.inf)
    idx_out = jnp.zeros((tm, LANES), F32)
    gate_out = jnp.zeros((tm, LANES), F32)
    top0 = None
    for k in range(TOP_K):
        mk = jnp.max(logits, axis=1, keepdims=True)
        ik = jnp.min(jnp.where(logits == mk, lane, float(LANES)), axis=1, keepdims=True)
        logits = jnp.where(lane == ik, -jnp.inf, logits)
        if top0 is None:
            top0 = mk
        idx_out = jnp.where(lane == k, ik, idx_out)
        gate_out = jnp.where(lane == k, jnp.exp(mk - top0), gate_out)
    gate_out = gate_out / jnp.sum(gate_out, axis=1, keepdims=True)
    outs[2][...] = idx_out.astype(I32)
    outs[3][...] = gate_out


def deepnorm(x, y, y_row_off, gate, ln_g, ln_b, alpha, rows_per_group, nxt=None, router=None,
             dest=None, dest_row_off=0, next_dtype=BF16):
    m, d = x.shape
    tm = min(256, rows_per_group)
    assert rows_per_group % tm == 0 and y_row_off % tm == 0 and dest_row_off % tm == 0
    bpg = rows_per_group // tm
    yo = y_row_off // tm
    do = dest_row_off // tm
    row = lambda i: (i, 0)
    grp = lambda i: (i // bpg, 0, 0)
    fix = lambda i: (0, 0)
    args = [x, y, gate, ln_g, ln_b]
    in_specs = [pl.BlockSpec((tm, d), row), pl.BlockSpec((tm, d), lambda i: (i + yo, 0)),
                pl.BlockSpec((None, 1, d), grp), pl.BlockSpec((1, d), fix), pl.BlockSpec((1, d), fix)]
    out_shape = [jax.ShapeDtypeStruct((m, d), F32)]
    out_specs = [pl.BlockSpec((tm, d), row)]
    n_experts = 0
    if nxt is not None:
        args += list(nxt)
        in_specs += [pl.BlockSpec((None, 1, d), grp)] * 2
    if router is not None:
        rw, rb, n_experts = router
        args += [rw, rb]
        in_specs += [pl.BlockSpec(rw.shape, fix), pl.BlockSpec(rb.shape, fix)]
    aliases = {}
    dst = lambda i: (i + do, 0)
    if nxt is not None:
        widths = [(d, next_dtype)] + ([(LANES, I32), (LANES, F32)] if router is not None else [])
        for k, (wd, dt) in enumerate(widths):
            if dest is not None:
                aliases[len(args)] = 1 + k
                args.append(dest[k])
                in_specs.append(pl.BlockSpec(memory_space=pl.ANY))
                out_shape.append(jax.ShapeDtypeStruct(dest[k].shape, dt))
            else:
                out_shape.append(jax.ShapeDtypeStruct((m, wd), dt))
            out_specs.append(pl.BlockSpec((tm, wd), dst))
    return pl.pallas_call(
        functools.partial(_ln_kernel, alpha=alpha, with_next=nxt is not None, with_router=router is not None,
                          n_experts=n_experts, next_dtype=next_dtype),
        grid=(m // tm,),
        in_specs=in_specs,
        out_specs=out_specs,
        out_shape=out_shape,
        input_output_aliases=aliases,
        compiler_params=_params(("arbitrary",)),
        name="deepnorm",
    )(*args)


def _gather_kernel(idx_ref, src_ref, o_ref, buf_ref, sem, *, rb):
    def copy(r, src_row):
        return pltpu.make_async_copy(src_ref.at[pl.ds(src_row, 1)], buf_ref.at[pl.ds(r, 1)], sem)

    for r in range(rb):
        copy(r, idx_ref[0, r]).start()
    for r in range(rb):
        copy(r, 0).wait()
    o_ref[...] = buf_ref[...].astype(o_ref.dtype)


def gather_rows(src, idx, out_dtype):
    p = idx.shape[0]
    d = src.shape[1]
    rb = GATHER_ROWS
    assert p % rb == 0
    return pl.pallas_call(
        functools.partial(_gather_kernel, rb=rb),
        grid=(p // rb,),
        in_specs=[pl.BlockSpec((None, 1, rb), lambda i: (i, 0, 0), memory_space=pltpu.SMEM),
                  pl.BlockSpec(memory_space=pl.ANY)],
        out_specs=pl.BlockSpec((rb, d), lambda i: (i, 0)),
        out_shape=jax.ShapeDtypeStruct((p, d), out_dtype),
        scratch_shapes=[pltpu.VMEM((rb, d), src.dtype), pltpu.SemaphoreType.DMA],
        compiler_params=_params(("arbitrary",)),
        name="moe_gather",
    )(idx.reshape(p // rb, 1, rb), src)


def _moe_up_kernel(be_ref, nb_ref, x_ref, wg_ref, wl_ref, bg_ref, bl_ref, o_ref, wgb_ref, wlb_ref):
    s = pl.program_id(1)
    first = jnp.logical_or(s == 0, be_ref[s] != be_ref[jnp.maximum(s - 1, 0)])

    @pl.when(first)
    def _():
        wgb_ref[...] = wg_ref[...].astype(BF16)
        wlb_ref[...] = wl_ref[...].astype(BF16)

    @pl.when(s < nb_ref[0])
    def _():
        x = x_ref[...]
        hg = jnp.dot(x, wgb_ref[...], preferred_element_type=F32) + bg_ref[...]
        hl = jnp.dot(x, wlb_ref[...], preferred_element_type=F32) + bl_ref[...]
        glu = jnp.minimum(hg, SWIGLU_LIMIT)
        lin = jnp.clip(hl, -SWIGLU_LIMIT, SWIGLU_LIMIT)
        act = glu * jax.nn.sigmoid(SWIGLU_ALPHA * glu) * (lin + 1.0)
        o_ref[...] = act.astype(o_ref.dtype)

    @pl.when(s >= nb_ref[0])
    def _():
        o_ref[...] = jnp.zeros_like(o_ref)


def moe_up(xs, w_up, b_up, layer, block_e, n_blocks_used):
    p, d = xs.shape
    f = w_up.shape[3] // 2
    tn = min(1024, f)
    nj = f // tn
    nb = p // MOE_ROWS
    grid_spec = pltpu.PrefetchScalarGridSpec(
        num_scalar_prefetch=2,
        grid=(nj, nb),
        in_specs=[pl.BlockSpec((MOE_ROWS, d), lambda j, s, be, nu: (s, 0)),
                  pl.BlockSpec((None, None, d, tn), lambda j, s, be, nu: (layer, be[s], 0, j)),
                  pl.BlockSpec((None, None, d, tn), lambda j, s, be, nu: (layer, be[s], 0, nj + j)),
                  pl.BlockSpec((None, None, 1, tn), lambda j, s, be, nu: (layer, be[s], 0, j)),
                  pl.BlockSpec((None, None, 1, tn), lambda j, s, be, nu: (layer, be[s], 0, nj + j))],
        out_specs=pl.BlockSpec((MOE_ROWS, tn), lambda j, s, be, nu: (s, j)),
        scratch_shapes=[pltpu.VMEM((d, tn), BF16), pltpu.VMEM((d, tn), BF16)])
    return pl.pallas_call(
        _moe_up_kernel,
        grid_spec=grid_spec,
        out_shape=jax.ShapeDtypeStruct((p, f), BF16),
        compiler_params=_params(("arbitrary", "arbitrary")),
        name="moe_up",
    )(block_e, n_blocks_used, xs, w_up, w_up, b_up, b_up)


def _moe_down_kernel(be_ref, nb_ref, a_ref, w_ref, b_ref, o_ref, wb_ref):
    s = pl.program_id(0)
    first = jnp.logical_or(s == 0, be_ref[s] != be_ref[jnp.maximum(s - 1, 0)])

    @pl.when(first)
    def _():
        wb_ref[...] = w_ref[...].astype(BF16)

    @pl.when(s < nb_ref[0])
    def _():
        o_ref[...] = jnp.dot(a_ref[...], wb_ref[...], preferred_element_type=F32) + b_ref[...]

    @pl.when(s >= nb_ref[0])
    def _():
        o_ref[...] = jnp.zeros_like(o_ref)


def moe_down(act, w_down, b_down, layer, block_e, n_blocks_used):
    p, f = act.shape
    d = w_down.shape[3]
    nb = p // MOE_ROWS
    grid_spec = pltpu.PrefetchScalarGridSpec(
        num_scalar_prefetch=2,
        grid=(nb,),
        in_specs=[pl.BlockSpec((MOE_ROWS, f), lambda s, be, nu: (s, 0)),
                  pl.BlockSpec((None, None, f, d), lambda s, be, nu: (layer, be[s], 0, 0)),
                  pl.BlockSpec((None, None, 1, d), lambda s, be, nu: (layer, be[s], 0, 0))],
        out_specs=pl.BlockSpec((MOE_ROWS, d), lambda s, be, nu: (s, 0)),
        scratch_shapes=[pltpu.VMEM((f, d), BF16)])
    return pl.pallas_call(
        _moe_down_kernel,
        grid_spec=grid_spec,
        out_shape=jax.ShapeDtypeStruct((p, d), F32),
        compiler_params=_params(("arbitrary",)),
        name="moe_down",
    )(block_e, n_blocks_used, act, w_down, b_down)


def _combine_kernel(idx_ref, gate_ref, rows_ref, o_ref, buf_ref, sem, *, tc):
    def copy(k, r, src_row):
        return pltpu.make_async_copy(rows_ref.at[pl.ds(src_row, 1)], buf_ref.at[k, pl.ds(r, 1)], sem)

    for k in range(TOP_K):
        for r in range(tc):
            copy(k, r, idx_ref[0, k * tc + r]).start()
    for k in range(TOP_K):
        for r in range(tc):
            copy(k, r, 0).wait()
    gate = gate_ref[...]
    acc = buf_ref[0] * gate[:, 0:1]
    for k in range(1, TOP_K):
        acc = acc + buf_ref[k] * gate[:, k:k + 1]
    o_ref[...] = acc


def moe_combine(rows, dest, gates):
    n = dest.shape[0]
    d = rows.shape[1]
    tc = COMBINE_TOKENS
    assert n % tc == 0
    idx = dest.reshape(n // tc, tc, TOP_K).transpose(0, 2, 1).reshape(n // tc, 1, TOP_K * tc)
    return pl.pallas_call(
        functools.partial(_combine_kernel, tc=tc),
        grid=(n // tc,),
        in_specs=[pl.BlockSpec((None, 1, TOP_K * tc), lambda i: (i, 0, 0), memory_space=pltpu.SMEM),
                  pl.BlockSpec((tc, LANES), lambda i: (i, 0)),
                  pl.BlockSpec(memory_space=pl.ANY)],
        out_specs=pl.BlockSpec((tc, d), lambda i: (i, 0)),
        out_shape=jax.ShapeDtypeStruct((n, d), F32),
        scratch_shapes=[pltpu.VMEM((TOP_K, tc, d), F32), pltpu.SemaphoreType.DMA],
        compiler_params=_params(("arbitrary",)),
        name="moe_combine",
    )(idx, gates, rows)


def moe_ffn(h_all, top_idx, top_gate, w_up, b_up, w_down, b_down, layer):
    n, d = h_all.shape
    n_exp = w_up.shape[1]
    n_pairs = n * TOP_K
    flat_e = top_idx[:, :TOP_K].reshape(-1)
    onehot = (flat_e[:, None] == jnp.arange(n_exp, dtype=I32)[None, :]).astype(I32)
    csum = jnp.cumsum(onehot, axis=0)
    counts = csum[-1]
    rank = jnp.sum((csum - onehot) * onehot, axis=1)
    padded = (counts + MOE_ROWS - 1) // MOE_ROWS * MOE_ROWS
    pend = jnp.cumsum(padded)
    pstart = pend - padded
    dest = pstart[flat_e] + rank
    nb = -(-n_pairs // MOE_ROWS) + n_exp
    p = nb * MOE_ROWS
    row_tok = jnp.zeros((p,), I32).at[dest].set(jnp.arange(n_pairs, dtype=I32) // TOP_K)
    block_e = jnp.minimum(jnp.searchsorted(pend, jnp.arange(nb, dtype=I32) * MOE_ROWS, side='right'),
                          n_exp - 1).astype(I32)
    n_used = (pend[-1] // MOE_ROWS).astype(I32).reshape(1)

    xs = gather_rows(h_all, row_tok, BF16)
    act = moe_up(xs, w_up, b_up, layer, block_e, n_used)
    rows = moe_down(act, w_down, b_down, layer, block_e, n_used)
    return moe_combine(rows, dest.reshape(n, TOP_K).astype(I32), top_gate)


def _pad_rows_front(a, rows):
    pad = rows - a.shape[1]
    return jnp.pad(a, ((0, 0), (pad, 0), (0, 0)))


def _pad_lanes(a):
    return jnp.pad(a, [(0, 0)] * (a.ndim - 1) + [(0, LANES - a.shape[-1])])


def kernel(x_prompt, x_sample, c_prompt, c_sample, cache_pool, cache_k, cache_v, cache_conv, state_C, state_n, state_m, ab_w_in, ab_w_pool, ab_pool_scale, ab_w_out, ml_w_in, ml_conv_w, ml_conv_b, ml_b_i, ml_b_f, ml_w_out, ada_w, ada_b, ln_g, ln_b, router_w, router_b, moe_w_up, moe_b_up, moe_w_down, moe_b_down):
    bp, tp, d = x_prompt.shape
    bs, ts, _ = x_sample.shape
    depth = ada_w.shape[0]
    alpha = (2 * depth) ** 0.25
    past, sb_heads, sb_dim = cache_k.shape[2:]
    sb_w = sb_heads * sb_dim
    pw = cache_pool.shape[-1]
    ml_heads, dk, dv = state_C.shape[2:]
    qkw = 2 * ml_heads * dk
    vw = ml_heads * dv
    n_exp = router_w.shape[-1]
    mp, ms = bp * tp, bs * ts
    n_tok = mp + ms

    n_c = bp + bs
    c_rows = -(-n_c // 16) * 16
    c_all = jnp.pad(jnp.concatenate([c_prompt, c_sample], axis=0), ((0, c_rows - n_c), (0, 0)))
    mod = adaln(c_all, ada_w, ada_b).reshape(depth, c_rows, 6, d)

    def terms(layer, which):
        rows = slice(0, bp) if which == 0 else slice(bp, bp + bs)
        return [mod[layer, rows, i, :][:, None, :] for i in range(6)]

    router_wp = _pad_lanes(router_w)
    router_bp = _pad_lanes(router_b)[:, None, :]
    b_up4 = moe_b_up[:, :, None, :]
    b_down4 = moe_b_down[:, :, None, :]

    tq = min(256, tp)
    tkn = LANES
    tri_n = max(tq, min(512, past), tkn)
    tri = (jnp.arange(tri_n)[:, None] > jnp.arange(tri_n)[None, :]).astype(BF16)

    xs_res = [x_prompt.reshape(mp, d), x_sample.reshape(ms, d)]
    trunk_rows = [tp, ts]
    trunk_b = [bp, bs]
    row_off = [0, mp]
    h_mix = [None, None]
    pools, ks, vs, convs, cs, ns, mstates = [[[], []] for _ in range(7)]

    for layer in range(depth):
        j = layer // 2
        t6 = [terms(layer, 0), terms(layer, 1)]
        if layer == 0:
            for w in range(2):
                sh_m, sc_m = t6[w][0], t6[w][1]
                h_mix[w] = modulate(xs_res[w].reshape(trunk_b[w], trunk_rows[w], d), sc_m, sh_m)
        y_mix = [None, None]
        if layer % 2 == 0:
            w_in = ab_w_in[j].astype(BF16)
            w_out = ab_w_out[j].astype(BF16)
            w_pool = ab_w_pool[j].astype(BF16)
            scale = ab_pool_scale[j][None, :]
            for w in range(2):
                bb, tt = trunk_b[w], trunk_rows[w]
                u, q, k, v = [matmul([h_mix[w]], w_in, [0], c0, wd)
                              for c0, wd in ((0, pw), (pw, sb_w), (pw + sb_w, sb_w), (pw + 2 * sb_w, sb_w))]
                u3 = u.reshape(bb, tt, pw)
                if w == 0:
                    hist = jnp.zeros((bb, POOL_HALO, pw), F32)
                    pool_out = pool_mix(u3, hist, w_pool, scale, 0)
                    att = attn_prompt(q.reshape(bb, tt, sb_w), k.reshape(bb, tt, sb_w), v.reshape(bb, tt, sb_w),
                                      tri[:tq, :tq], sb_heads, tq)
                    pools[w].append(u3[:, tt - (POOL_HALO - 1):])
                else:
                    hist = _pad_rows_front(cache_pool[j], POOL_HALO)
                    pool_out = pool_mix(u3, hist, w_pool, scale, past)
                    z_pool = jnp.concatenate([cache_pool[j], u3], axis=1)
                    pools[w].append(z_pool[:, -(POOL_HALO - 1):])
                    q4 = q.reshape(bb, tt, sb_heads, sb_dim)
                    eye = jnp.eye(sb_heads, dtype=F32)
                    q_bd = jnp.einsum('bihd,hg->bhigd', q4, eye).reshape(bb, sb_heads * tt, sb_w).astype(BF16)
                    k_new = jnp.pad(k.reshape(bb, tt, sb_w), ((0, 0), (0, tkn - tt), (0, 0)))
                    v_new = jnp.pad(v.reshape(bb, tt, sb_w), ((0, 0), (0, tkn - tt), (0, 0)))
                    o_bd = attn_sample(q_bd, k_new, v_new, cache_k[j].reshape(bb, past, sb_w),
                                       cache_v[j].reshape(bb, past, sb_w), tri, tt, sb_dim)
                    o5 = o_bd.reshape(bb, sb_heads, tt, sb_heads, sb_dim)
                    att = jnp.einsum('bhihd->bihd', o5).reshape(bb * tt, sb_w).astype(BF16)
                ks[w].append(k.reshape(bb, tt, sb_heads, sb_dim))
                vs[w].append(v.reshape(bb, tt, sb_heads, sb_dim))
                y_mix[w] = matmul([pool_out, att], w_out, [0, pw], 0, d)
        else:
            w_in = ml_w_in[j]
            o1 = qkw
            o2 = o1 + vw
            o3 = o2 + vw
            w_main = w_in[:, :o3].astype(BF16)
            w_gate = _pad_lanes(w_in[:, o3:]).astype(BF16)
            w_out = ml_w_out[j].astype(BF16)
            conv_w = jnp.pad(ml_conv_w[j], ((0, CONV_HALO - CONV_W), (0, 0)))
            conv_b = ml_conv_b[j][None, :]
            gate_b = _pad_lanes(jnp.concatenate([ml_b_i[j], ml_b_f[j]]))[None, :]
            for w in range(2):
                bb, tt = trunk_b[w], trunk_rows[w]
                qk_pre = matmul([h_mix[w]], w_main, [0], 0, qkw).reshape(bb, tt, qkw)
                v_in = matmul([h_mix[w]], w_main, [0], o1, vw).reshape(bb, tt, vw)
                o_pre = matmul([h_mix[w]], w_main, [0], o2, vw).reshape(bb, tt, vw)
                gates = matmul([h_mix[w]], w_gate, [0], 0, LANES, tn=LANES).reshape(bb, tt, LANES)
                if w == 0:
                    hist = jnp.zeros((bb, CONV_HALO, qkw), F32)
                    c0 = jnp.zeros((bb, ml_heads, dk, dv), F32)
                    n0 = jnp.zeros((bb, ml_heads, dk), F32)
                    m0 = jnp.zeros((bb, 1, LANES), F32)
                    chunk = min(256, tt)
                    hid, c1, n1, m1 = mlstm_mix(qk_pre, v_in, o_pre, gates, conv_w, conv_b, gate_b, hist,
                                                c0, n0, m0, chunk, chunk)
                    convs[w].append(qk_pre[:, tt - (CONV_W - 1):])
                else:
                    hist = _pad_rows_front(cache_conv[j], CONV_HALO)
                    m0 = _pad_lanes(state_m[j])[:, None, :]
                    chunk = -(-tt // LANES) * LANES
                    padt = lambda a: jnp.pad(a, ((0, 0), (0, chunk - tt), (0, 0)))
                    hid, c1, n1, m1 = mlstm_mix(padt(qk_pre), padt(v_in), padt(o_pre), padt(gates), conv_w, conv_b,
                                                gate_b, hist, state_C[j], state_n[j], m0, chunk, tt)
                    hid = hid[:, :tt]
                    z_conv = jnp.concatenate([cache_conv[j], qk_pre], axis=1)
                    convs[w].append(z_conv[:, -(CONV_W - 1):])
                cs[w].append(c1)
                ns[w].append(n1)
                mstates[w].append(m1[:, 0, :ml_heads])
                y_mix[w] = matmul([hid.reshape(bb * tt, vw)], w_out, [0], 0, d)

        lg1, lb1 = ln_g[layer, 0][None, :], ln_b[layer, 0][None, :]
        lg2, lb2 = ln_g[layer, 1][None, :], ln_b[layer, 1][None, :]
        router = (router_wp[layer], router_bp[layer], n_exp)
        shared = (jnp.zeros((n_tok, d), F32), jnp.zeros((n_tok, LANES), I32), jnp.zeros((n_tok, LANES), F32))
        x_mid = [None, None]
        for w in range(2):
            g_m, sh_f, sc_f = t6[w][2], t6[w][3], t6[w][4]
            x_mid[w], *shared = deepnorm(xs_res[w], y_mix[w], 0, g_m, lg1, lb1, alpha, trunk_rows[w],
                                         nxt=(sc_f, sh_f), router=router, dest=tuple(shared),
                                         dest_row_off=row_off[w], next_dtype=F32)
        h_all, top_idx, top_gate = shared
        y_ffn = moe_ffn(h_all, top_idx, top_gate, moe_w_up, b_up4, moe_w_down, b_down4, layer)
        for w in range(2):
            g_f = t6[w][5]
            if layer + 1 < depth:
                t_next = terms(layer + 1, w)
                xs_res[w], h_mix[w] = deepnorm(x_mid[w], y_ffn, row_off[w], g_f, lg2, lb2, alpha, trunk_rows[w],
                                               nxt=(t_next[1], t_next[0]))
            else:
                (xs_res[w],) = deepnorm(x_mid[w], y_ffn, row_off[w], g_f, lg2, lb2, alpha, trunk_rows[w])

    outs = []
    for w in range(2):
        outs.append((xs_res[w].reshape(trunk_b[w], trunk_rows[w], d), jnp.stack(pools[w]), jnp.stack(ks[w]),
                     jnp.stack(vs[w]), jnp.stack(convs[w]), jnp.stack(cs[w]), jnp.stack(ns[w]),
                     jnp.stack(mstates[w])))
    (y_p, *rest_p), (y_s, *rest_s) = outs
    return (y_p, y_s, *rest_p, *rest_s)
```

```python
import functools

import jax
import jax.numpy as jnp
from jax import lax
from jax.experimental import pallas as pl
from jax.experimental.pallas import tpu as pltpu

F32 = jnp.float32
BF16 = jnp.bfloat16
I32 = jnp.int32

TOP_K = 4
POOL_WINDOWS = (2, 4, 8, 16)
POOL_HALO = 16
CONV_W = 4
CONV_HALO = 8
SWIGLU_LIMIT = 7.0
SWIGLU_ALPHA = 1.702
LN_EPS = 1e-5
LANES = 128
MXU_COLS = 256
MOE_ROWS = 512
GATHER_ROWS = 256
COMBINE_TOKENS = 64
VMEM_LIMIT = 56 * 1024 * 1024
NEG_BIG = -1e30

_HIGHEST = lax.Precision.HIGHEST


def _params(sem, vmem=None):
    return pltpu.CompilerParams(dimension_semantics=sem, vmem_limit_bytes=vmem or VMEM_LIMIT)


def _log_sigmoid(z):
    return jnp.minimum(z, 0.0) - jnp.log(1.0 + jnp.exp(-jnp.abs(z)))


def _adaln_kernel(c_ref, w_ref, b_ref, o_ref):
    c = c_ref[...]
    s = (c * jax.nn.sigmoid(c)).astype(BF16)
    o_ref[...] = jnp.dot(s, w_ref[...].astype(BF16), preferred_element_type=F32) + b_ref[...]


def adaln(c_all, ada_w, ada_b):
    nl, d, n = ada_w.shape
    c = c_all.shape[0]
    tn = min(1024, n)
    return pl.pallas_call(
        _adaln_kernel,
        grid=(nl, n // tn),
        in_specs=[pl.BlockSpec((c, d), lambda l, j: (0, 0)),
                  pl.BlockSpec((None, d, tn), lambda l, j: (l, 0, j)),
                  pl.BlockSpec((None, 1, tn), lambda l, j: (l, 0, j))],
        out_specs=pl.BlockSpec((None, c, tn), lambda l, j: (l, 0, j)),
        out_shape=jax.ShapeDtypeStruct((nl, c, n), F32),
        compiler_params=_params(("arbitrary", "arbitrary")),
        name="adaln",
    )(c_all, ada_w, ada_b.reshape(nl, 1, n))


def _modulate_kernel(x_ref, sc_ref, sh_ref, o_ref):
    o_ref[...] = (x_ref[...] * (1.0 + sc_ref[...]) + sh_ref[...]).astype(o_ref.dtype)


def modulate(x, sc, sh):
    g, r, d = x.shape
    tr = min(512, r)
    out = pl.pallas_call(
        _modulate_kernel,
        grid=(g, r // tr),
        in_specs=[pl.BlockSpec((None, tr, d), lambda b, i: (b, i, 0)),
                  pl.BlockSpec((None, 1, d), lambda b, i: (b, 0, 0)),
                  pl.BlockSpec((None, 1, d), lambda b, i: (b, 0, 0))],
        out_specs=pl.BlockSpec((None, tr, d), lambda b, i: (b, i, 0)),
        out_shape=jax.ShapeDtypeStruct((g, r, d), BF16),
        compiler_params=_params(("arbitrary", "arbitrary")),
        name="modulate",
    )(x, sc, sh)
    return out.reshape(g * r, d)


def _matmul_kernel(*refs, n_a):
    a_refs, w_refs, o_ref = refs[:n_a], refs[n_a:2 * n_a], refs[2 * n_a]
    acc = jnp.dot(a_refs[0][...], w_refs[0][...], preferred_element_type=F32)
    for a, w in zip(a_refs[1:], w_refs[1:]):
        acc = acc + jnp.dot(a[...], w[...], preferred_element_type=F32)
    o_ref[...] = acc.astype(o_ref.dtype)


def matmul(a_list, w, row_offs, col_off, n_cols, out_dtype=F32, tm=512, tn=1024):
    m = a_list[0].shape[0]
    tm = min(tm, m)
    tn = min(tn, n_cols)
    assert m % tm == 0 and n_cols % tn == 0 and col_off % tn == 0
    n_a = len(a_list)
    in_specs = [pl.BlockSpec((tm, a.shape[1]), lambda j, i: (i, 0)) for a in a_list]
    for a, ro in zip(a_list, row_offs):
        ka = a.shape[1]
        assert ro % ka == 0
        in_specs.append(pl.BlockSpec((ka, tn), functools.partial(
            lambda j, i, rb, cb: (rb, cb + j), rb=ro // ka, cb=col_off // tn)))
    return pl.pallas_call(
        functools.partial(_matmul_kernel, n_a=n_a),
        grid=(n_cols // tn, m // tm),
        in_specs=in_specs,
        out_specs=pl.BlockSpec((tm, tn), lambda j, i: (i, j)),
        out_shape=jax.ShapeDtypeStruct((m, n_cols), out_dtype),
        compiler_params=_params(("arbitrary", "arbitrary")),
        name="matmul",
    )(*a_list, *([w] * n_a))


def _pool_kernel(u_ref, hist_ref, w_ref, scale_ref, o_ref, z_ref, *, tt, pg, pos0):
    ti = pl.program_id(1)

    @pl.when(ti == 0)
    def _():
        z_ref[0:POOL_HALO, :] = hist_ref[...]

    @pl.when(ti > 0)
    def _():
        z_ref[0:POOL_HALO, :] = z_ref[tt:tt + POOL_HALO, :]

    z_ref[POOL_HALO:POOL_HALO + tt, :] = u_ref[...]
    pos = pos0 + ti * tt + lax.broadcasted_iota(I32, (tt, 1), 0)
    for g, w in enumerate(POOL_WINDOWS):
        cols = slice(g * pg, (g + 1) * pg)
        u_g = z_ref[POOL_HALO:POOL_HALO + tt, cols]
        acc = u_g
        for j in range(1, w):
            acc = acc + z_ref[POOL_HALO - j:POOL_HALO - j + tt, cols]
        cnt = jnp.minimum(w, pos + 1).astype(F32)
        diff = (acc / cnt - u_g).astype(BF16)
        y = jnp.dot(diff, w_ref[g], preferred_element_type=F32) * scale_ref[:, cols]
        o_ref[:, cols] = y.astype(o_ref.dtype)


def pool_mix(u, hist, w_pool, scale, pos0):
    b, t, pw = u.shape
    ng, pg, _ = w_pool.shape
    tt = min(512, t)
    assert tt >= POOL_HALO and t % tt == 0 and ng == len(POOL_WINDOWS)
    out = pl.pallas_call(
        functools.partial(_pool_kernel, tt=tt, pg=pg, pos0=pos0),
        grid=(b, t // tt),
        in_specs=[pl.BlockSpec((None, tt, pw), lambda i, j: (i, j, 0)),
                  pl.BlockSpec((None, POOL_HALO, pw), lambda i, j: (i, 0, 0)),
                  pl.BlockSpec((ng, pg, pg), lambda i, j: (0, 0, 0)),
                  pl.BlockSpec((1, pw), lambda i, j: (0, 0))],
        out_specs=pl.BlockSpec((None, tt, pw), lambda i, j: (i, j, 0)),
        out_shape=jax.ShapeDtypeStruct((b, t, pw), BF16),
        scratch_shapes=[pltpu.VMEM((tt + POOL_HALO, pw), F32)],
        compiler_params=_params(("arbitrary", "arbitrary")),
        name="pool_mix",
    )(u, hist, w_pool, scale)
    return out.reshape(b * t, pw)


def _sb_block(z, lsz_mask, tri, v_bf, r_prev):
    lsz = _log_sigmoid(z)
    lk = lsz - z
    if lsz_mask is not None:
        lk = jnp.where(lsz_mask, lk, 0.0)
    hi = lk.astype(BF16)
    lo = (lk - hi.astype(F32)).astype(BF16)
    after = (jnp.dot(hi, tri, preferred_element_type=F32)
             + jnp.dot(lo, tri, preferred_element_type=F32))
    a = jnp.exp(lsz + after + r_prev)
    if lsz_mask is not None:
        a = jnp.where(lsz_mask, a, 0.0)
    pv = jnp.dot(a.astype(BF16), v_bf, preferred_element_type=F32)
    return pv, jnp.sum(lk, axis=1, keepdims=True)


def _attn_prompt_kernel(q_ref, k_ref, v_ref, tri_ref, o_ref, kb_ref, vb_ref, r_ref, acc_ref, *, scale, tq, tk):
    qi = pl.program_id(2)

    @pl.when(qi == 0)
    def _():
        kb_ref[...] = k_ref[...].astype(BF16)
        vb_ref[...] = v_ref[...].astype(BF16)

    q = q_ref[...].astype(BF16)
    r_ref[...] = jnp.zeros_like(r_ref)
    acc_ref[...] = jnp.zeros_like(acc_ref)

    def key_block(k0, masked):
        for sub in reversed(range(tq // tk)):
            ks = pl.multiple_of(k0 + sub * tk, tk)
            z = lax.dot_general(q, kb_ref[pl.ds(ks, tk), :], (((1,), (1,)), ((), ())),
                                preferred_element_type=F32) * scale
            mask = None
            if masked:
                row = lax.broadcasted_iota(I32, (tq, tk), 0)
                col = lax.broadcasted_iota(I32, (tq, tk), 1) + sub * tk
                mask = col < row
            pv, lk_sum = _sb_block(z, mask, tri_ref[...], vb_ref[pl.ds(ks, tk), :], r_ref[...])
            acc_ref[...] += pv
            r_ref[...] += lk_sum

    key_block(qi * tq, True)

    def body(i, carry):
        key_block((qi - 1 - i) * tq, False)
        return carry

    lax.fori_loop(0, qi, body, 0)
    o_ref[...] = acc_ref[...].astype(o_ref.dtype)


def attn_prompt(q, k, v, tri, heads, tq):
    b, t, w = q.shape
    d = w // heads
    tk = tri.shape[0]
    assert t % tq == 0 and tq % tk == 0
    kv_spec = pl.BlockSpec((None, t, d), lambda bi, h, qi: (bi, 0, h))
    out = pl.pallas_call(
        functools.partial(_attn_prompt_kernel, scale=d ** -0.5, tq=tq, tk=tk),
        grid=(b, heads, t // tq),
        in_specs=[pl.BlockSpec((None, tq, d), lambda bi, h, qi: (bi, qi, h)),
                  kv_spec, kv_spec,
                  pl.BlockSpec((tk, tk), lambda bi, h, qi: (0, 0))],
        out_specs=pl.BlockSpec((None, tq, d), lambda bi, h, qi: (bi, qi, h)),
        out_shape=jax.ShapeDtypeStruct((b, t, w), BF16),
        scratch_shapes=[pltpu.VMEM((t, d), BF16), pltpu.VMEM((t, d), BF16),
                        pltpu.VMEM((tq, 1), F32), pltpu.VMEM((tq, d), F32)],
        compiler_params=_params(("arbitrary",) * 3),
        name="attn_prompt",
    )(q, k, v, tri)
    return out.reshape(b * t, w)


def _attn_sample_kernel(q_ref, kn_ref, vn_ref, kp_ref, vp_ref, tri_ref, o_ref, r_ref, *, scale, ts, tkn, tkp, heads):
    j = pl.program_id(1)
    rows = q_ref.shape[0]

    def block(k, v, tk, masked):
        z = lax.dot_general(q_ref[...], k, (((1,), (1,)), ((), ())), preferred_element_type=F32) * scale
        mask = None
        if masked:
            row = lax.rem(lax.broadcasted_iota(I32, (rows, tk), 0), ts)
            col = lax.broadcasted_iota(I32, (rows, tk), 1)
            mask = col < row
        pv, lk_sum = _sb_block(z, mask, tri_ref[0:tk, 0:tk], v, r_ref[...])
        o_ref[...] += pv
        r_ref[...] += lk_sum

    def heads_on_lanes(ref):
        return jnp.concatenate([ref[pl.ds(h, tkp, stride=heads), :].astype(BF16) for h in range(heads)], axis=1)

    @pl.when(j == 0)
    def _():
        r_ref[...] = jnp.zeros_like(r_ref)
        o_ref[...] = jnp.zeros_like(o_ref)
        block(kn_ref[...].astype(BF16), vn_ref[...].astype(BF16), tkn, True)

    @pl.when(j > 0)
    def _():
        block(heads_on_lanes(kp_ref), heads_on_lanes(vp_ref), tkp, False)


def attn_sample(q_bd, k_new, v_new, k_past, v_past, tri, ts, d):
    b, rows, w = q_bd.shape
    heads = w // d
    tkn = k_new.shape[1]
    past = k_past.shape[1] // heads
    tkp = min(512, past)
    assert past % tkp == 0 and tri.shape[0] >= max(tkn, tkp)
    n_past = past // tkp
    past_spec = pl.BlockSpec((None, tkp * heads, d), lambda bi, j: (bi, n_past - jnp.maximum(j, 1), 0))
    new_spec = pl.BlockSpec((None, tkn, w), lambda bi, j: (bi, 0, 0))
    return pl.pallas_call(
        functools.partial(_attn_sample_kernel, scale=d ** -0.5, ts=ts, tkn=tkn, tkp=tkp, heads=heads),
        grid=(b, 1 + n_past),
        in_specs=[pl.BlockSpec((None, rows, w), lambda bi, j: (bi, 0, 0)),
                  new_spec, new_spec, past_spec, past_spec,
                  pl.BlockSpec(tri.shape, lambda bi, j: (0, 0))],
        out_specs=pl.BlockSpec((None, rows, w), lambda bi, j: (bi, 0, 0)),
        out_shape=jax.ShapeDtypeStruct((b, rows, w), F32),
        scratch_shapes=[pltpu.VMEM((rows, 1), F32)],
        compiler_params=_params(("arbitrary", "arbitrary")),
        name="attn_sample",
    )(q_bd, k_new, v_new, k_past, v_past, tri)


def _mlstm_kernel(qk_ref, v_ref, op_ref, g_ref, cw_ref, cb_ref, gb_ref, hist_ref, c0_ref, n0_ref, m0_ref,
                  tril_ref, triu_ref,
                  hid_ref, c1_ref, n1_ref, m1_ref,
                  z_ref, c_ref, n_ref, m_ref, *, L, H, Dk, Dv, n_valid, kscale):
    c = pl.program_id(1)
    qw = H * Dk

    @pl.when(c == 0)
    def _():
        z_ref[0:CONV_HALO, :] = hist_ref[...]
        c_ref[...] = c0_ref[...]
        n_ref[...] = n0_ref[...]
        m_ref[...] = m0_ref[...]

    @pl.when(c > 0)
    def _():
        z_ref[0:CONV_HALO, :] = z_ref[L:L + CONV_HALO, :]

    z_ref[CONV_HALO:CONV_HALO + L, :] = qk_ref[...]
    y = cb_ref[...]
    for j in range(CONV_W):
        r0 = CONV_HALO - (CONV_W - 1) + j
        y = y + z_ref[r0:r0 + L, :] * cw_ref[j:j + 1, :]
    qk = y * jax.nn.sigmoid(y)

    g = g_ref[...] + gb_ref[...]
    lane = lax.broadcasted_iota(I32, (L, LANES), 1)
    gl = jnp.where(jnp.logical_and(lane >= H, lane < 2 * H), _log_sigmoid(g), g)
    if n_valid < L:
        row = lax.broadcasted_iota(I32, (L, LANES), 0)
        gl = jnp.where(row < n_valid, gl, jnp.where(lane < H, NEG_BIG, 0.0))
    glt = gl.T
    b_col = jnp.dot(tril_ref[...], gl, preferred_element_type=F32, precision=_HIGHEST)
    b_row = jnp.dot(glt, triu_ref[...], preferred_element_type=F32, precision=_HIGHEST)
    trow = lax.broadcasted_iota(I32, (L, L), 0)
    tcol = lax.broadcasted_iota(I32, (L, L), 1)
    causal = tcol <= trow

    for h in range(H):
        m_prev = m_ref[:, h:h + 1]
        bc = b_col[:, H + h:H + h + 1]
        br = b_row[H + h:H + h + 1, :]
        igc = gl[:, h:h + 1]
        igr = glt[h:h + 1, :]
        dm = jnp.where(causal, bc - br + igr, -jnp.inf)
        inter = bc + m_prev
        m_t = jnp.maximum(inter, jnp.max(dm, axis=1, keepdims=True))
        qh = qk[:, h * Dk:(h + 1) * Dk]
        kh = qk[:, qw + h * Dk:qw + (h + 1) * Dk] * kscale
        qb = qh.astype(BF16)
        vb = v_ref[:, h * Dv:(h + 1) * Dv].astype(BF16)
        s = jnp.exp(dm - m_t) * lax.dot_general(qb, kh.astype(BF16), (((1,), (1,)), ((), ())),
                                                 preferred_element_type=F32)
        decay = jnp.exp(inter - m_t)
        ch = c_ref[h]
        nh = n_ref[h:h + 1, :]
        num = (jnp.dot(s.astype(BF16), vb, preferred_element_type=F32)
               + decay * jnp.dot(qb, ch.astype(BF16), preferred_element_type=F32))
        den = jnp.sum(s, axis=1, keepdims=True) + decay * jnp.sum(qh * nh, axis=1, keepdims=True)
        hval = num / jnp.maximum(jnp.abs(den), jnp.exp(-m_t))
        og = jax.nn.sigmoid(op_ref[:, h * Dv:(h + 1) * Dv])
        hid_ref[:, h * Dv:(h + 1) * Dv] = (og * hval).astype(hid_ref.dtype)

        b_last = bc[L - 1:L, :]
        gc = b_last - bc + igc
        m_new = jnp.maximum(b_last + m_prev, jnp.max(gc, axis=0, keepdims=True))
        kw = kh * jnp.exp(gc - m_new)
        keep = jnp.exp(b_last + m_prev - m_new)
        c_ref[h] = keep * ch + jnp.dot(kw.T.astype(BF16), vb, preferred_element_type=F32)
        n_ref[h:h + 1, :] = keep * nh + jnp.sum(kw, axis=0, keepdims=True)
        m_ref[:, h:h + 1] = m_new

    @pl.when(c == pl.num_programs(1) - 1)
    def _():
        c1_ref[...] = c_ref[...]
        n1_ref[...] = n_ref[...]
        m1_ref[...] = m_ref[...]


def mlstm_mix(qk_pre, v, o_pre, gates, conv_w, conv_b, gate_b, hist, c0, n0, m0, L, n_valid):
    b, t, qk2 = qk_pre.shape
    _, hh, dk, dv = c0.shape
    vw = hh * dv
    assert t % L == 0
    tril = jnp.tril(jnp.ones((L, L), F32))
    row3 = lambda i, j: (i, j, 0)
    fix3 = lambda i, j: (i, 0, 0)
    fix2 = lambda i, j: (0, 0)
    return pl.pallas_call(
        functools.partial(_mlstm_kernel, L=L, H=hh, Dk=dk, Dv=dv, n_valid=n_valid, kscale=dk ** -0.5),
        grid=(b, t // L),
        in_specs=[pl.BlockSpec((None, L, qk2), row3),
                  pl.BlockSpec((None, L, vw), row3),
                  pl.BlockSpec((None, L, vw), row3),
                  pl.BlockSpec((None, L, LANES), row3),
                  pl.BlockSpec(conv_w.shape, fix2),
                  pl.BlockSpec(conv_b.shape, fix2),
                  pl.BlockSpec(gate_b.shape, fix2),
                  pl.BlockSpec((None, CONV_HALO, qk2), fix3),
                  pl.BlockSpec((None, hh, dk, dv), lambda i, j: (i, 0, 0, 0)),
                  pl.BlockSpec((None, hh, dk), fix3),
                  pl.BlockSpec((None, 1, LANES), fix3),
                  pl.BlockSpec((L, L), fix2),
                  pl.BlockSpec((L, L), fix2)],
        out_specs=[pl.BlockSpec((None, L, vw), row3),
                   pl.BlockSpec((None, hh, dk, dv), lambda i, j: (i, 0, 0, 0)),
                   pl.BlockSpec((None, hh, dk), fix3),
                   pl.BlockSpec((None, 1, LANES), fix3)],
        out_shape=[jax.ShapeDtypeStruct((b, t, vw), BF16),
                   jax.ShapeDtypeStruct(c0.shape, F32),
                   jax.ShapeDtypeStruct(n0.shape, F32),
                   jax.ShapeDtypeStruct(m0.shape, F32)],
        scratch_shapes=[pltpu.VMEM((L + CONV_HALO, qk2), F32),
                        pltpu.VMEM((hh, dk, dv), F32),
                        pltpu.VMEM((hh, dk), F32),
                        pltpu.VMEM((1, LANES), F32)],
        compiler_params=_params(("arbitrary", "arbitrary")),
        name="mlstm_mix",
    )(qk_pre, v, o_pre, gates, conv_w, conv_b, gate_b, hist, c0, n0, m0, tril, tril.T)


def _ln_kernel(*refs, alpha, with_next, with_router, n_experts, next_dtype):
    x_ref, y_ref, gate_ref, lg_ref, lb_ref = refs[:5]
    pos = 5
    if with_next:
        sc_ref, sh_ref = refs[pos:pos + 2]
        pos += 2
    if with_router:
        rw_ref, rb_ref = refs[pos:pos + 2]
        pos += 2
    n_out = 1 + int(with_next) + 2 * int(with_router)
    outs = refs[len(refs) - n_out:]
    xo_ref = outs[0]

    v = alpha * x_ref[...] + (1.0 + gate_ref[...]) * y_ref[...]
    mu = jnp.mean(v, axis=-1, keepdims=True)
    vc = v - mu
    var = jnp.mean(vc * vc, axis=-1, keepdims=True)
    xn = vc * lax.rsqrt(var + LN_EPS) * lg_ref[...] + lb_ref[...]
    xo_ref[...] = xn
    if not with_next:
        return
    h = xn * (1.0 + sc_ref[...]) + sh_ref[...]
    outs[1][...] = h.astype(next_dtype)
    if not with_router:
        return
    tm = h.shape[0]
    logits = jnp.dot(h.astype(BF16), rw_ref[...].astype(BF16), preferred_element_type=F32) + rb_ref[...]
    lane = lax.broadcasted_iota(I32, (tm, LANES), 1).astype(F32)
    logits = jnp.where(lane < n_experts, logits, -jnp.inf)
    idx_out = jnp.zeros((tm, LANES), F32)
    gate_out = jnp.zeros((tm, LANES), F32)
    top0 = None
    for k in range(TOP_K):
        mk = jnp.max(logits, axis=1, keepdims=True)
        ik = jnp.min(jnp.where(logits == mk, lane, float(LANES)), axis=1, keepdims=True)
        logits = jnp.where(lane == ik, -jnp.inf, logits)
        if top0 is None:
            top0 = mk
        idx_out = jnp.where(lane == k, ik, idx_out)
        gate_out = jnp.where(lane == k, jnp.exp(mk - top0), gate_out)
    gate_out = gate_out / jnp.sum(gate_out, axis=1, keepdims=True)
    outs[2][...] = idx_out.astype(I32)
    outs[3][...] = gate_out


def deepnorm(x, y, y_row_off, gate, ln_g, ln_b, alpha, rows_per_group, nxt=None, router=None,
             dest=None, dest_row_off=0, next_dtype=BF16):
    m, d = x.shape
    tm = min(256, rows_per_group)
    assert rows_per_group % tm == 0 and y_row_off % tm == 0 and dest_row_off % tm == 0
    bpg = rows_per_group // tm
    yo = y_row_off // tm
    do = dest_row_off // tm
    row = lambda i: (i, 0)
    grp = lambda i: (i // bpg, 0, 0)
    fix = lambda i: (0, 0)
    args = [x, y, gate, ln_g, ln_b]
    in_specs = [pl.BlockSpec((tm, d), row), pl.BlockSpec((tm, d), lambda i: (i + yo, 0)),
                pl.BlockSpec((None, 1, d), grp), pl.BlockSpec((1, d), fix), pl.BlockSpec((1, d), fix)]
    out_shape = [jax.ShapeDtypeStruct((m, d), F32)]
    out_specs = [pl.BlockSpec((tm, d), row)]
    n_experts = 0
    if nxt is not None:
        args += list(nxt)
        in_specs += [pl.BlockSpec((None, 1, d), grp)] * 2
    if router is not None:
        rw, rb, n_experts = router
        args += [rw, rb]
        in_specs += [pl.BlockSpec(rw.shape, fix), pl.BlockSpec(rb.shape, fix)]
    aliases = {}
    dst = lambda i: (i + do, 0)
    if nxt is not None:
        widths = [(d, next_dtype)] + ([(LANES, I32), (LANES, F32)] if router is not None else [])
        for k, (wd, dt) in enumerate(widths):
            if dest is not None:
                aliases[len(args)] = 1 + k
                args.append(dest[k])
                in_specs.append(pl.BlockSpec(memory_space=pl.ANY))
                out_shape.append(jax.ShapeDtypeStruct(dest[k].shape, dt))
            else:
                out_shape.append(jax.ShapeDtypeStruct((m, wd), dt))
            out_specs.append(pl.BlockSpec((tm, wd), dst))
    return pl.pallas_call(
        functools.partial(_ln_kernel, alpha=alpha, with_next=nxt is not None, with_router=router is not None,
                          n_experts=n_experts, next_dtype=next_dtype),
        grid=(m // tm,),
        in_specs=in_specs,
        out_specs=out_specs,
        out_shape=out_shape,
        input_output_aliases=aliases,
        compiler_params=_params(("arbitrary",)),
        name="deepnorm",
    )(*args)


def _gather_kernel(idx_ref, idx_next_ref, src_ref, o_ref, buf_ref, sem, *, rb):
    i = pl.program_id(0)
    slot = lax.rem(i, 2)

    def copy(sl, r, src_row):
        return pltpu.make_async_copy(src_ref.at[pl.ds(src_row, 1)], buf_ref.at[sl, pl.ds(r, 1)], sem.at[sl])

    def start_block(rows_ref, sl):
        for r in range(rb):
            copy(sl, r, rows_ref[0, r]).start(priority=r % 2)

    @pl.when(i == 0)
    def _():
        start_block(idx_ref, 0)

    @pl.when(i + 1 < pl.num_programs(0))
    def _():
        start_block(idx_next_ref, 1 - slot)

    for r in range(rb):
        copy(slot, r, 0).wait()
    o_ref[...] = buf_ref[slot].astype(o_ref.dtype)


def gather_rows(src, idx, out_dtype):
    p = idx.shape[0]
    d = src.shape[1]
    rb = GATHER_ROWS
    assert p % rb == 0
    nblk = p // rb
    idx3 = idx.reshape(nblk, 1, rb)
    return pl.pallas_call(
        functools.partial(_gather_kernel, rb=rb),
        grid=(nblk,),
        in_specs=[pl.BlockSpec((None, 1, rb), lambda i: (i, 0, 0), memory_space=pltpu.SMEM),
                  pl.BlockSpec((None, 1, rb), lambda i: (jnp.minimum(i + 1, nblk - 1), 0, 0),
                               memory_space=pltpu.SMEM),
                  pl.BlockSpec(memory_space=pl.ANY)],
        out_specs=pl.BlockSpec((rb, d), lambda i: (i, 0)),
        out_shape=jax.ShapeDtypeStruct((p, d), out_dtype),
        scratch_shapes=[pltpu.VMEM((2, rb, d), src.dtype), pltpu.SemaphoreType.DMA((2,))],
        compiler_params=_params(("arbitrary",)),
        name="moe_gather",
    )(idx3, idx3, src)


def _moe_up_kernel(be_ref, nb_ref, x_ref, wg_ref, wl_ref, bg_ref, bl_ref, o_ref, wgb_ref, wlb_ref):
    s = pl.program_id(1)
    first = jnp.logical_or(s == 0, be_ref[s] != be_ref[jnp.maximum(s - 1, 0)])

    @pl.when(first)
    def _():
        wgb_ref[...] = wg_ref[...].astype(BF16)
        wlb_ref[...] = wl_ref[...].astype(BF16)

    @pl.when(s < nb_ref[0])
    def _():
        x = x_ref[...]
        tn = o_ref.shape[1]
        for c0 in range(0, tn, MXU_COLS):
            cols = slice(c0, min(c0 + MXU_COLS, tn))
            hg = jnp.dot(x, wgb_ref[:, cols], preferred_element_type=F32) + bg_ref[:, cols]
            hl = jnp.dot(x, wlb_ref[:, cols], preferred_element_type=F32) + bl_ref[:, cols]
            glu = jnp.minimum(hg, SWIGLU_LIMIT)
            lin = jnp.clip(hl, -SWIGLU_LIMIT, SWIGLU_LIMIT)
            act = glu * jax.nn.sigmoid(SWIGLU_ALPHA * glu) * (lin + 1.0)
            o_ref[:, cols] = act.astype(o_ref.dtype)

    @pl.when(s >= nb_ref[0])
    def _():
        o_ref[...] = jnp.zeros_like(o_ref)


def moe_up(xs, w_up, b_up, layer, block_e, n_blocks_used):
    p, d = xs.shape
    f = w_up.shape[3] // 2
    tn = min(1024, f)
    nj = f // tn
    nb = p // MOE_ROWS
    grid_spec = pltpu.PrefetchScalarGridSpec(
        num_scalar_prefetch=2,
        grid=(nj, nb),
        in_specs=[pl.BlockSpec((MOE_ROWS, d), lambda j, s, be, nu: (s, 0)),
                  pl.BlockSpec((None, None, d, tn), lambda j, s, be, nu: (layer, be[s], 0, j)),
                  pl.BlockSpec((None, None, d, tn), lambda j, s, be, nu: (layer, be[s], 0, nj + j)),
                  pl.BlockSpec((None, None, 1, tn), lambda j, s, be, nu: (layer, be[s], 0, j)),
                  pl.BlockSpec((None, None, 1, tn), lambda j, s, be, nu: (layer, be[s], 0, nj + j))],
        out_specs=pl.BlockSpec((MOE_ROWS, tn), lambda j, s, be, nu: (s, j)),
        scratch_shapes=[pltpu.VMEM((d, tn), BF16), pltpu.VMEM((d, tn), BF16)])
    return pl.pallas_call(
        _moe_up_kernel,
        grid_spec=grid_spec,
        out_shape=jax.ShapeDtypeStruct((p, f), BF16),
        compiler_params=_params(("arbitrary", "arbitrary")),
        name="moe_up",
    )(block_e, n_blocks_used, xs, w_up, w_up, b_up, b_up)


def _moe_down_kernel(be_ref, nb_ref, a_ref, w_ref, b_ref, o_ref, wb_ref):
    s = pl.program_id(1)
    first = jnp.logical_or(s == 0, be_ref[s] != be_ref[jnp.maximum(s - 1, 0)])

    @pl.when(first)
    def _():
        wb_ref[...] = w_ref[...].astype(BF16)

    @pl.when(s < nb_ref[0])
    def _():
        o_ref[...] = jnp.dot(a_ref[...], wb_ref[...], preferred_element_type=F32) + b_ref[...]

    @pl.when(s >= nb_ref[0])
    def _():
        o_ref[...] = jnp.zeros_like(o_ref)


def moe_down(act, w_down, b_down, layer, block_e, n_blocks_used):
    p, f = act.shape
    d = w_down.shape[3]
    nb = p // MOE_ROWS
    tn = min(1024, d)
    grid_spec = pltpu.PrefetchScalarGridSpec(
        num_scalar_prefetch=2,
        grid=(d // tn, nb),
        in_specs=[pl.BlockSpec((MOE_ROWS, f), lambda j, s, be, nu: (s, 0)),
                  pl.BlockSpec((None, None, f, tn), lambda j, s, be, nu: (layer, be[s], 0, j)),
                  pl.BlockSpec((None, None, 1, tn), lambda j, s, be, nu: (layer, be[s], 0, j))],
        out_specs=pl.BlockSpec((MOE_ROWS, tn), lambda j, s, be, nu: (s, j)),
        scratch_shapes=[pltpu.VMEM((f, tn), BF16)])
    return pl.pallas_call(
        _moe_down_kernel,
        grid_spec=grid_spec,
        out_shape=jax.ShapeDtypeStruct((p, d), F32),
        compiler_params=_params(("arbitrary", "arbitrary")),
        name="moe_down",
    )(block_e, n_blocks_used, act, w_down, b_down)


def _combine_kernel(idx_ref, idx_next_ref, gate_ref, rows_ref, o_ref, buf_ref, sem, *, tc):
    i = pl.program_id(0)
    slot = lax.rem(i, 2)

    def copy(sl, k, r, src_row):
        return pltpu.make_async_copy(rows_ref.at[pl.ds(src_row, 1)], buf_ref.at[sl, k, pl.ds(r, 1)], sem.at[sl])

    def start_block(rows_idx_ref, sl):
        for k in range(TOP_K):
            for r in range(tc):
                copy(sl, k, r, rows_idx_ref[0, k * tc + r]).start(priority=r % 2)

    @pl.when(i == 0)
    def _():
        start_block(idx_ref, 0)

    @pl.when(i + 1 < pl.num_programs(0))
    def _():
        start_block(idx_next_ref, 1 - slot)

    for k in range(TOP_K):
        for r in range(tc):
            copy(slot, k, r, 0).wait()
    gate = gate_ref[...]
    acc = buf_ref[slot, 0] * gate[:, 0:1]
    for k in range(1, TOP_K):
        acc = acc + buf_ref[slot, k] * gate[:, k:k + 1]
    o_ref[...] = acc


def moe_combine(rows, dest, gates):
    n = dest.shape[0]
    d = rows.shape[1]
    tc = COMBINE_TOKENS
    assert n % tc == 0
    nblk = n // tc
    idx = dest.reshape(nblk, tc, TOP_K).transpose(0, 2, 1).reshape(nblk, 1, TOP_K * tc)
    return pl.pallas_call(
        functools.partial(_combine_kernel, tc=tc),
        grid=(nblk,),
        in_specs=[pl.BlockSpec((None, 1, TOP_K * tc), lambda i: (i, 0, 0), memory_space=pltpu.SMEM),
                  pl.BlockSpec((None, 1, TOP_K * tc), lambda i: (jnp.minimum(i + 1, nblk - 1), 0, 0),
                               memory_space=pltpu.SMEM),
                  pl.BlockSpec((tc, LANES), lambda i: (i, 0)),
                  pl.BlockSpec(memory_space=pl.ANY)],
        out_specs=pl.BlockSpec((tc, d), lambda i: (i, 0)),
        out_shape=jax.ShapeDtypeStruct((n, d), F32),
        scratch_shapes=[pltpu.VMEM((2, TOP_K, tc, d), F32), pltpu.SemaphoreType.DMA((2,))],
        compiler_params=_params(("arbitrary",)),
        name="moe_combine",
    )(idx, idx, gates, rows)


def moe_ffn(h_all, top_idx, top_gate, w_up, b_up, w_down, b_down, layer):
    n, d = h_all.shape
    n_exp = w_up.shape[1]
    n_pairs = n * TOP_K
    flat_e = top_idx[:, :TOP_K].reshape(-1)
    blk = 256
    assert n_pairs % blk == 0
    onehot = (flat_e[:, None] == jnp.arange(n_exp, dtype=I32)[None, :]).astype(F32).reshape(n_pairs // blk, blk, n_exp)
    within = jnp.einsum('ts,bse->bte', jnp.tril(jnp.ones((blk, blk), F32)), onehot)
    blk_tot = within[:, -1, :]
    blk_end = jnp.cumsum(blk_tot, axis=0)
    csum = within + (blk_end - blk_tot)[:, None, :]
    counts = blk_end[-1].astype(I32)
    rank = jnp.sum((csum - onehot) * onehot, axis=-1).reshape(-1).astype(I32)
    padded = (counts + MOE_ROWS - 1) // MOE_ROWS * MOE_ROWS
    pend = jnp.cumsum(padded)
    pstart = pend - padded
    dest = pstart[flat_e] + rank
    nb = -(-n_pairs // MOE_ROWS) + n_exp
    p = nb * MOE_ROWS
    row_tok = jnp.zeros((p,), I32).at[dest].set(jnp.arange(n_pairs, dtype=I32) // TOP_K)
    block_e = jnp.minimum(jnp.searchsorted(pend, jnp.arange(nb, dtype=I32) * MOE_ROWS, side='right'),
                          n_exp - 1).astype(I32)
    n_used = (pend[-1] // MOE_ROWS).astype(I32).reshape(1)

    xs = gather_rows(h_all, row_tok, BF16)
    act = moe_up(xs, w_up, b_up, layer, block_e, n_used)
    rows = moe_down(act, w_down, b_down, layer, block_e, n_used)
    return moe_combine(rows, dest.reshape(n, TOP_K).astype(I32), top_gate)


def _pad_rows_front(a, rows):
    pad = rows - a.shape[1]
    return jnp.pad(a, ((0, 0), (pad, 0), (0, 0)))


def _pad_lanes(a):
    return jnp.pad(a, [(0, 0)] * (a.ndim - 1) + [(0, LANES - a.shape[-1])])


def kernel(x_prompt, x_sample, c_prompt, c_sample, cache_pool, cache_k, cache_v, cache_conv, state_C, state_n, state_m, ab_w_in, ab_w_pool, ab_pool_scale, ab_w_out, ml_w_in, ml_conv_w, ml_conv_b, ml_b_i, ml_b_f, ml_w_out, ada_w, ada_b, ln_g, ln_b, router_w, router_b, moe_w_up, moe_b_up, moe_w_down, moe_b_down):
    bp, tp, d = x_prompt.shape
    bs, ts, _ = x_sample.shape
    depth = ada_w.shape[0]
    alpha = (2 * depth) ** 0.25
    past, sb_heads, sb_dim = cache_k.shape[2:]
    sb_w = sb_heads * sb_dim
    pw = cache_pool.shape[-1]
    ml_heads, dk, dv = state_C.shape[2:]
    qkw = 2 * ml_heads * dk
    vw = ml_heads * dv
    n_exp = router_w.shape[-1]
    mp, ms = bp * tp, bs * ts
    n_tok = mp + ms

    n_c = bp + bs
    c_rows = -(-n_c // 16) * 16
    c_all = jnp.pad(jnp.concatenate([c_prompt, c_sample], axis=0), ((0, c_rows - n_c), (0, 0)))
    mod = adaln(c_all, ada_w, ada_b).reshape(depth, c_rows, 6, d)

    def terms(layer, which):
        rows = slice(0, bp) if which == 0 else slice(bp, bp + bs)
        return [mod[layer, rows, i, :][:, None, :] for i in range(6)]

    router_wp = _pad_lanes(router_w)
    router_bp = _pad_lanes(router_b)[:, None, :]
    b_up4 = moe_b_up[:, :, None, :]
    b_down4 = moe_b_down[:, :, None, :]

    tq = min(512, tp // 2)
    tk_att = min(256, tq)
    tkn = LANES
    tri_n = max(tk_att, min(512, past), tkn)
    tri = (jnp.arange(tri_n)[:, None] > jnp.arange(tri_n)[None, :]).astype(BF16)

    xs_res = [x_prompt.reshape(mp, d), x_sample.reshape(ms, d)]
    trunk_rows = [tp, ts]
    trunk_b = [bp, bs]
    row_off = [0, mp]
    h_mix = [None, None]
    pools, ks, vs, convs, cs, ns, mstates = [[[], []] for _ in range(7)]

    for layer in range(depth):
        j = layer // 2
        t6 = [terms(layer, 0), terms(layer, 1)]
        if layer == 0:
            for w in range(2):
                sh_m, sc_m = t6[w][0], t6[w][1]
                h_mix[w] = modulate(xs_res[w].reshape(trunk_b[w], trunk_rows[w], d), sc_m, sh_m)
        y_mix = [None, None]
        if layer % 2 == 0:
            w_in = ab_w_in[j].astype(BF16)
            w_out = ab_w_out[j].astype(BF16)
            w_pool = ab_w_pool[j].astype(BF16)
            scale = ab_pool_scale[j][None, :]
            for w in range(2):
                bb, tt = trunk_b[w], trunk_rows[w]
                u, q, k, v = [matmul([h_mix[w]], w_in, [0], c0, wd)
                              for c0, wd in ((0, pw), (pw, sb_w), (pw + sb_w, sb_w), (pw + 2 * sb_w, sb_w))]
                u3 = u.reshape(bb, tt, pw)
                if w == 0:
                    hist = jnp.zeros((bb, POOL_HALO, pw), F32)
                    pool_out = pool_mix(u3, hist, w_pool, scale, 0)
                    att = attn_prompt(q.reshape(bb, tt, sb_w), k.reshape(bb, tt, sb_w), v.reshape(bb, tt, sb_w),
                                      tri[:tk_att, :tk_att], sb_heads, tq)
                    pools[w].append(u3[:, tt - (POOL_HALO - 1):])
                else:
                    hist = _pad_rows_front(cache_pool[j], POOL_HALO)
                    pool_out = pool_mix(u3, hist, w_pool, scale, past)
                    z_pool = jnp.concatenate([cache_pool[j], u3], axis=1)
                    pools[w].append(z_pool[:, -(POOL_HALO - 1):])
                    q4 = q.reshape(bb, tt, sb_heads, sb_dim)
                    eye = jnp.eye(sb_heads, dtype=F32)
                    q_bd = jnp.einsum('bihd,hg->bhigd', q4, eye).reshape(bb, sb_heads * tt, sb_w).astype(BF16)
                    k_new = jnp.pad(k.reshape(bb, tt, sb_w), ((0, 0), (0, tkn - tt), (0, 0)))
                    v_new = jnp.pad(v.reshape(bb, tt, sb_w), ((0, 0), (0, tkn - tt), (0, 0)))
                    o_bd = attn_sample(q_bd, k_new, v_new, cache_k[j].reshape(bb, past * sb_heads, sb_dim),
                                       cache_v[j].reshape(bb, past * sb_heads, sb_dim), tri, tt, sb_dim)
                    o5 = o_bd.reshape(bb, sb_heads, tt, sb_heads, sb_dim)
                    att = jnp.einsum('bhihd->bihd', o5).reshape(bb * tt, sb_w).astype(BF16)
                ks[w].append(k.reshape(bb, tt, sb_heads, sb_dim))
                vs[w].append(v.reshape(bb, tt, sb_heads, sb_dim))
                y_mix[w] = matmul([pool_out, att], w_out, [0, pw], 0, d)
        else:
            w_in = ml_w_in[j]
            o1 = qkw
            o2 = o1 + vw
            o3 = o2 + vw
            w_main = w_in[:, :o3].astype(BF16)
            w_gate = _pad_lanes(w_in[:, o3:]).astype(BF16)
            w_out = ml_w_out[j].astype(BF16)
            conv_w = jnp.pad(ml_conv_w[j], ((0, CONV_HALO - CONV_W), (0, 0)))
            conv_b = ml_conv_b[j][None, :]
            gate_b = _pad_lanes(jnp.concatenate([ml_b_i[j], ml_b_f[j]]))[None, :]
            for w in range(2):
                bb, tt = trunk_b[w], trunk_rows[w]
                qk_pre = matmul([h_mix[w]], w_main, [0], 0, qkw).reshape(bb, tt, qkw)
                v_in = matmul([h_mix[w]], w_main, [0], o1, vw).reshape(bb, tt, vw)
                o_pre = matmul([h_mix[w]], w_main, [0], o2, vw).reshape(bb, tt, vw)
                gates = matmul([h_mix[w]], w_gate, [0], 0, LANES, tn=LANES).reshape(bb, tt, LANES)
                if w == 0:
                    hist = jnp.zeros((bb, CONV_HALO, qkw), F32)
                    c0 = jnp.zeros((bb, ml_heads, dk, dv), F32)
                    n0 = jnp.zeros((bb, ml_heads, dk), F32)
                    m0 = jnp.zeros((bb, 1, LANES), F32)
                    chunk = min(256, tt)
                    hid, c1, n1, m1 = mlstm_mix(qk_pre, v_in, o_pre, gates, conv_w, conv_b, gate_b, hist,
                                                c0, n0, m0, chunk, chunk)
                    convs[w].append(qk_pre[:, tt - (CONV_W - 1):])
                else:
                    hist = _pad_rows_front(cache_conv[j], CONV_HALO)
                    m0 = _pad_lanes(state_m[j])[:, None, :]
                    chunk = -(-tt // LANES) * LANES
                    padt = lambda a: jnp.pad(a, ((0, 0), (0, chunk - tt), (0, 0)))
                    hid, c1, n1, m1 = mlstm_mix(padt(qk_pre), padt(v_in), padt(o_pre), padt(gates), conv_w, conv_b,
                                                gate_b, hist, state_C[j], state_n[j], m0, chunk, tt)
                    hid = hid[:, :tt]
                    z_conv = jnp.concatenate([cache_conv[j], qk_pre], axis=1)
                    convs[w].append(z_conv[:, -(CONV_W - 1):])
                cs[w].append(c1)
                ns[w].append(n1)
                mstates[w].append(m1[:, 0, :ml_heads])
                y_mix[w] = matmul([hid.reshape(bb * tt, vw)], w_out, [0], 0, d)

        lg1, lb1 = ln_g[layer, 0][None, :], ln_b[layer, 0][None, :]
        lg2, lb2 = ln_g[layer, 1][None, :], ln_b[layer, 1][None, :]
        router = (router_wp[layer], router_bp[layer], n_exp)
        shared = (jnp.zeros((n_tok, d), F32), jnp.zeros((n_tok, LANES), I32), jnp.zeros((n_tok, LANES), F32))
        x_mid = [None, None]
        for w in range(2):
            g_m, sh_f, sc_f = t6[w][2], t6[w][3], t6[w][4]
            x_mid[w], *shared = deepnorm(xs_res[w], y_mix[w], 0, g_m, lg1, lb1, alpha, trunk_rows[w],
                                         nxt=(sc_f, sh_f), router=router, dest=tuple(shared),
                                         dest_row_off=row_off[w], next_dtype=F32)
        h_all, top_idx, top_gate = shared
        y_ffn = moe_ffn(h_all, top_idx, top_gate, moe_w_up, b_up4, moe_w_down, b_down4, layer)
        for w in range(2):
            g_f = t6[w][5]
            if layer + 1 < depth:
                t_next = terms(layer + 1, w)
                xs_res[w], h_mix[w] = deepnorm(x_mid[w], y_ffn, row_off[w], g_f, lg2, lb2, alpha, trunk_rows[w],
                                               nxt=(t_next[1], t_next[0]))
            else:
                (xs_res[w],) = deepnorm(x_mid[w], y_ffn, row_off[w], g_f, lg2, lb2, alpha, trunk_rows[w])

    outs = []
    for w in range(2):
        outs.append((xs_res[w].reshape(trunk_b[w], trunk_rows[w], d), jnp.stack(pools[w]), jnp.stack(ks[w]),
                     jnp.stack(vs[w]), jnp.stack(convs[w]), jnp.stack(cs[w]), jnp.stack(ns[w]),
                     jnp.stack(mstates[w])))
    (y_p, *rest_p), (y_s, *rest_s) = outs
    return (y_p, y_s, *rest_p, *rest_s)
```

```python
import functools

import jax
import jax.numpy as jnp
from jax import lax
from jax.experimental import pallas as pl
from jax.experimental.pallas import tpu as pltpu

F32 = jnp.float32
BF16 = jnp.bfloat16
I32 = jnp.int32

TOP_K = 4
POOL_WINDOWS = (2, 4, 8, 16)
POOL_HALO = 16
CONV_W = 4
CONV_HALO = 8
SWIGLU_LIMIT = 7.0
SWIGLU_ALPHA = 1.702
LN_EPS = 1e-5
LANES = 128
MXU_COLS = 256
MOE_ROWS = 512
GATHER_ROWS = 256
COMBINE_TOKENS = 64
VMEM_LIMIT = 56 * 1024 * 1024
NEG_BIG = -1e30

_HIGHEST = lax.Precision.HIGHEST


def _params(sem, vmem=None):
    return pltpu.CompilerParams(dimension_semantics=sem, vmem_limit_bytes=vmem or VMEM_LIMIT)


def _log_sigmoid(z):
    return jnp.minimum(z, 0.0) - jnp.log(1.0 + jnp.exp(-jnp.abs(z)))


def _adaln_kernel(c_ref, w_ref, b_ref, o_ref):
    c = c_ref[...]
    s = (c * jax.nn.sigmoid(c)).astype(BF16)
    o_ref[...] = jnp.dot(s, w_ref[...].astype(BF16), preferred_element_type=F32) + b_ref[...]


def adaln(c_all, ada_w, ada_b):
    nl, d, n = ada_w.shape
    c = c_all.shape[0]
    tn = min(1024, n)
    return pl.pallas_call(
        _adaln_kernel,
        grid=(nl, n // tn),
        in_specs=[pl.BlockSpec((c, d), lambda l, j: (0, 0)),
                  pl.BlockSpec((None, d, tn), lambda l, j: (l, 0, j)),
                  pl.BlockSpec((None, 1, tn), lambda l, j: (l, 0, j))],
        out_specs=pl.BlockSpec((None, c, tn), lambda l, j: (l, 0, j)),
        out_shape=jax.ShapeDtypeStruct((nl, c, n), F32),
        compiler_params=_params(("arbitrary", "arbitrary")),
        name="adaln",
    )(c_all, ada_w, ada_b.reshape(nl, 1, n))


def _modulate_kernel(x_ref, sc_ref, sh_ref, o_ref):
    o_ref[...] = (x_ref[...] * (1.0 + sc_ref[...]) + sh_ref[...]).astype(o_ref.dtype)


def modulate(x, sc, sh):
    g, r, d = x.shape
    tr = min(512, r)
    out = pl.pallas_call(
        _modulate_kernel,
        grid=(g, r // tr),
        in_specs=[pl.BlockSpec((None, tr, d), lambda b, i: (b, i, 0)),
                  pl.BlockSpec((None, 1, d), lambda b, i: (b, 0, 0)),
                  pl.BlockSpec((None, 1, d), lambda b, i: (b, 0, 0))],
        out_specs=pl.BlockSpec((None, tr, d), lambda b, i: (b, i, 0)),
        out_shape=jax.ShapeDtypeStruct((g, r, d), BF16),
        compiler_params=_params(("arbitrary", "arbitrary")),
        name="modulate",
    )(x, sc, sh)
    return out.reshape(g * r, d)


def _matmul_kernel(*refs, n_a):
    a_refs, w_refs, o_ref = refs[:n_a], refs[n_a:2 * n_a], refs[2 * n_a]
    acc = jnp.dot(a_refs[0][...], w_refs[0][...], preferred_element_type=F32)
    for a, w in zip(a_refs[1:], w_refs[1:]):
        acc = acc + jnp.dot(a[...], w[...], preferred_element_type=F32)
    o_ref[...] = acc.astype(o_ref.dtype)


def matmul(a_list, w, row_offs, col_off, n_cols, out_dtype=F32, tm=512, tn=1024):
    m = a_list[0].shape[0]
    tm = min(tm, m)
    tn = min(tn, n_cols)
    assert m % tm == 0 and n_cols % tn == 0 and col_off % tn == 0
    n_a = len(a_list)
    in_specs = [pl.BlockSpec((tm, a.shape[1]), lambda j, i: (i, 0)) for a in a_list]
    for a, ro in zip(a_list, row_offs):
        ka = a.shape[1]
        assert ro % ka == 0
        in_specs.append(pl.BlockSpec((ka, tn), functools.partial(
            lambda j, i, rb, cb: (rb, cb + j), rb=ro // ka, cb=col_off // tn)))
    return pl.pallas_call(
        functools.partial(_matmul_kernel, n_a=n_a),
        grid=(n_cols // tn, m // tm),
        in_specs=in_specs,
        out_specs=pl.BlockSpec((tm, tn), lambda j, i: (i, j)),
        out_shape=jax.ShapeDtypeStruct((m, n_cols), out_dtype),
        compiler_params=_params(("arbitrary", "arbitrary")),
        name="matmul",
    )(*a_list, *([w] * n_a))


def _pool_kernel(u_ref, hist_ref, w_ref, scale_ref, o_ref, z_ref, *, tt, pg, pos0):
    ti = pl.program_id(1)

    @pl.when(ti == 0)
    def _():
        z_ref[0:POOL_HALO, :] = hist_ref[...]

    @pl.when(ti > 0)
    def _():
        z_ref[0:POOL_HALO, :] = z_ref[tt:tt + POOL_HALO, :]

    z_ref[POOL_HALO:POOL_HALO + tt, :] = u_ref[...]
    pos = pos0 + ti * tt + lax.broadcasted_iota(I32, (tt, 1), 0)
    for g, w in enumerate(POOL_WINDOWS):
        cols = slice(g * pg, (g + 1) * pg)
        u_g = z_ref[POOL_HALO:POOL_HALO + tt, cols]
        acc = u_g
        for j in range(1, w):
            acc = acc + z_ref[POOL_HALO - j:POOL_HALO - j + tt, cols]
        cnt = jnp.minimum(w, pos + 1).astype(F32)
        diff = (acc / cnt - u_g).astype(BF16)
        y = jnp.dot(diff, w_ref[g], preferred_element_type=F32) * scale_ref[:, cols]
        o_ref[:, cols] = y.astype(o_ref.dtype)


def pool_mix(u, hist, w_pool, scale, pos0):
    b, t, pw = u.shape
    ng, pg, _ = w_pool.shape
    tt = min(512, t)
    assert tt >= POOL_HALO and t % tt == 0 and ng == len(POOL_WINDOWS)
    out = pl.pallas_call(
        functools.partial(_pool_kernel, tt=tt, pg=pg, pos0=pos0),
        grid=(b, t // tt),
        in_specs=[pl.BlockSpec((None, tt, pw), lambda i, j: (i, j, 0)),
                  pl.BlockSpec((None, POOL_HALO, pw), lambda i, j: (i, 0, 0)),
                  pl.BlockSpec((ng, pg, pg), lambda i, j: (0, 0, 0)),
                  pl.BlockSpec((1, pw), lambda i, j: (0, 0))],
        out_specs=pl.BlockSpec((None, tt, pw), lambda i, j: (i, j, 0)),
        out_shape=jax.ShapeDtypeStruct((b, t, pw), BF16),
        scratch_shapes=[pltpu.VMEM((tt + POOL_HALO, pw), F32)],
        compiler_params=_params(("arbitrary", "arbitrary")),
        name="pool_mix",
    )(u, hist, w_pool, scale)
    return out.reshape(b * t, pw)


def _sb_block(z, lsz_mask, tri, v_bf, r_prev):
    lsz = _log_sigmoid(z)
    lk = lsz - z
    if lsz_mask is not None:
        lk = jnp.where(lsz_mask, lk, 0.0)
    hi = lk.astype(BF16)
    lo = (lk - hi.astype(F32)).astype(BF16)
    after = (jnp.dot(hi, tri, preferred_element_type=F32)
             + jnp.dot(lo, tri, preferred_element_type=F32))
    a = jnp.exp(lsz + after + r_prev)
    if lsz_mask is not None:
        a = jnp.where(lsz_mask, a, 0.0)
    pv = jnp.dot(a.astype(BF16), v_bf, preferred_element_type=F32)
    return pv, jnp.sum(lk, axis=1, keepdims=True)


def _attn_prompt_kernel(q_ref, k_ref, v_ref, tri_ref, o_ref, kb_ref, vb_ref, r_ref, acc_ref, *, scale, tq, tk):
    qi = pl.program_id(2)

    @pl.when(qi == 0)
    def _():
        kb_ref[...] = k_ref[...].astype(BF16)
        vb_ref[...] = v_ref[...].astype(BF16)

    q = q_ref[...].astype(BF16)
    r_ref[...] = jnp.zeros_like(r_ref)
    acc_ref[...] = jnp.zeros_like(acc_ref)

    def key_block(k0, masked):
        for sub in reversed(range(tq // tk)):
            ks = pl.multiple_of(k0 + sub * tk, tk)
            z = lax.dot_general(q, kb_ref[pl.ds(ks, tk), :], (((1,), (1,)), ((), ())),
                                preferred_element_type=F32) * scale
            mask = None
            if masked:
                row = lax.broadcasted_iota(I32, (tq, tk), 0)
                col = lax.broadcasted_iota(I32, (tq, tk), 1) + sub * tk
                mask = col < row
            pv, lk_sum = _sb_block(z, mask, tri_ref[...], vb_ref[pl.ds(ks, tk), :], r_ref[...])
            acc_ref[...] += pv
            r_ref[...] += lk_sum

    key_block(qi * tq, True)

    def body(i, carry):
        key_block((qi - 1 - i) * tq, False)
        return carry

    lax.fori_loop(0, qi, body, 0)
    o_ref[...] = acc_ref[...].astype(o_ref.dtype)


def attn_prompt(q, k, v, tri, heads, tq):
    b, t, w = q.shape
    d = w // heads
    tk = tri.shape[0]
    assert t % tq == 0 and tq % tk == 0
    kv_spec = pl.BlockSpec((None, t, d), lambda bi, h, qi: (bi, 0, h))
    out = pl.pallas_call(
        functools.partial(_attn_prompt_kernel, scale=d ** -0.5, tq=tq, tk=tk),
        grid=(b, heads, t // tq),
        in_specs=[pl.BlockSpec((None, tq, d), lambda bi, h, qi: (bi, qi, h)),
                  kv_spec, kv_spec,
                  pl.BlockSpec((tk, tk), lambda bi, h, qi: (0, 0))],
        out_specs=pl.BlockSpec((None, tq, d), lambda bi, h, qi: (bi, qi, h)),
        out_shape=jax.ShapeDtypeStruct((b, t, w), BF16),
        scratch_shapes=[pltpu.VMEM((t, d), BF16), pltpu.VMEM((t, d), BF16),
                        pltpu.VMEM((tq, 1), F32), pltpu.VMEM((tq, d), F32)],
        compiler_params=_params(("arbitrary",) * 3),
        name="attn_prompt",
    )(q, k, v, tri)
    return out.reshape(b * t, w)


def _attn_sample_kernel(q_ref, kn_ref, vn_ref, kp_ref, vp_ref, tri_ref, o_ref, r_ref, *, scale, ts, tkn, tkp, heads):
    j = pl.program_id(1)
    rows = q_ref.shape[0]

    def block(k, v, tk, masked):
        z = lax.dot_general(q_ref[...], k, (((1,), (1,)), ((), ())), preferred_element_type=F32) * scale
        mask = None
        if masked:
            row = lax.rem(lax.broadcasted_iota(I32, (rows, tk), 0), ts)
            col = lax.broadcasted_iota(I32, (rows, tk), 1)
            mask = col < row
        pv, lk_sum = _sb_block(z, mask, tri_ref[0:tk, 0:tk], v, r_ref[...])
        o_ref[...] += pv
        r_ref[...] += lk_sum

    def heads_on_lanes(ref):
        return jnp.concatenate([ref[pl.ds(h, tkp, stride=heads), :].astype(BF16) for h in range(heads)], axis=1)

    @pl.when(j == 0)
    def _():
        r_ref[...] = jnp.zeros_like(r_ref)
        o_ref[...] = jnp.zeros_like(o_ref)
        block(kn_ref[...].astype(BF16), vn_ref[...].astype(BF16), tkn, True)

    @pl.when(j > 0)
    def _():
        block(heads_on_lanes(kp_ref), heads_on_lanes(vp_ref), tkp, False)


def attn_sample(q_bd, k_new, v_new, k_past, v_past, tri, ts, d):
    b, rows, w = q_bd.shape
    heads = w // d
    tkn = k_new.shape[1]
    past = k_past.shape[1] // heads
    tkp = min(512, past)
    assert past % tkp == 0 and tri.shape[0] >= max(tkn, tkp)
    n_past = past // tkp
    past_spec = pl.BlockSpec((None, tkp * heads, d), lambda bi, j: (bi, n_past - jnp.maximum(j, 1), 0))
    new_spec = pl.BlockSpec((None, tkn, w), lambda bi, j: (bi, 0, 0))
    return pl.pallas_call(
        functools.partial(_attn_sample_kernel, scale=d ** -0.5, ts=ts, tkn=tkn, tkp=tkp, heads=heads),
        grid=(b, 1 + n_past),
        in_specs=[pl.BlockSpec((None, rows, w), lambda bi, j: (bi, 0, 0)),
                  new_spec, new_spec, past_spec, past_spec,
                  pl.BlockSpec(tri.shape, lambda bi, j: (0, 0))],
        out_specs=pl.BlockSpec((None, rows, w), lambda bi, j: (bi, 0, 0)),
        out_shape=jax.ShapeDtypeStruct((b, rows, w), F32),
        scratch_shapes=[pltpu.VMEM((rows, 1), F32)],
        compiler_params=_params(("arbitrary", "arbitrary")),
        name="attn_sample",
    )(q_bd, k_new, v_new, k_past, v_past, tri)


def _mlstm_kernel(qk_ref, v_ref, op_ref, g_ref, cw_ref, cb_ref, gb_ref, hist_ref, c0_ref, n0_ref, m0_ref,
                  tril_ref, triu_ref,
                  hid_ref, c1_ref, n1_ref, m1_ref,
                  z_ref, c_ref, n_ref, m_ref, *, L, H, Dk, Dv, n_valid, kscale):
    c = pl.program_id(1)
    qw = H * Dk

    @pl.when(c == 0)
    def _():
        z_ref[0:CONV_HALO, :] = hist_ref[...]
        c_ref[...] = c0_ref[...]
        n_ref[...] = n0_ref[...]
        m_ref[...] = m0_ref[...]

    @pl.when(c > 0)
    def _():
        z_ref[0:CONV_HALO, :] = z_ref[L:L + CONV_HALO, :]

    z_ref[CONV_HALO:CONV_HALO + L, :] = qk_ref[...]
    y = cb_ref[...]
    for j in range(CONV_W):
        r0 = CONV_HALO - (CONV_W - 1) + j
        y = y + z_ref[r0:r0 + L, :] * cw_ref[j:j + 1, :]
    qk = y * jax.nn.sigmoid(y)

    g = g_ref[...] + gb_ref[...]
    lane = lax.broadcasted_iota(I32, (L, LANES), 1)
    gl = jnp.where(jnp.logical_and(lane >= H, lane < 2 * H), _log_sigmoid(g), g)
    if n_valid < L:
        row = lax.broadcasted_iota(I32, (L, LANES), 0)
        gl = jnp.where(row < n_valid, gl, jnp.where(lane < H, NEG_BIG, 0.0))
    glt = gl.T
    b_col = jnp.dot(tril_ref[...], gl, preferred_element_type=F32, precision=_HIGHEST)
    b_row = jnp.dot(glt, triu_ref[...], preferred_element_type=F32, precision=_HIGHEST)
    trow = lax.broadcasted_iota(I32, (L, L), 0)
    tcol = lax.broadcasted_iota(I32, (L, L), 1)
    causal = tcol <= trow

    for h in range(H):
        m_prev = m_ref[:, h:h + 1]
        bc = b_col[:, H + h:H + h + 1]
        br = b_row[H + h:H + h + 1, :]
        igc = gl[:, h:h + 1]
        igr = glt[h:h + 1, :]
        dm = jnp.where(causal, bc - br + igr, -jnp.inf)
        inter = bc + m_prev
        m_t = jnp.maximum(inter, jnp.max(dm, axis=1, keepdims=True))
        qh = qk[:, h * Dk:(h + 1) * Dk]
        kh = qk[:, qw + h * Dk:qw + (h + 1) * Dk] * kscale
        qb = qh.astype(BF16)
        vb = v_ref[:, h * Dv:(h + 1) * Dv].astype(BF16)
        s = jnp.exp(dm - m_t) * lax.dot_general(qb, kh.astype(BF16), (((1,), (1,)), ((), ())),
                                                 preferred_element_type=F32)
        decay = jnp.exp(inter - m_t)
        ch = c_ref[h]
        nh = n_ref[h:h + 1, :]
        num = (jnp.dot(s.astype(BF16), vb, preferred_element_type=F32)
               + decay * jnp.dot(qb, ch.astype(BF16), preferred_element_type=F32))
        den = jnp.sum(s, axis=1, keepdims=True) + decay * jnp.sum(qh * nh, axis=1, keepdims=True)
        hval = num / jnp.maximum(jnp.abs(den), jnp.exp(-m_t))
        og = jax.nn.sigmoid(op_ref[:, h * Dv:(h + 1) * Dv])
        hid_ref[:, h * Dv:(h + 1) * Dv] = (og * hval).astype(hid_ref.dtype)

        b_last = bc[L - 1:L, :]
        gc = b_last - bc + igc
        m_new = jnp.maximum(b_last + m_prev, jnp.max(gc, axis=0, keepdims=True))
        kw = kh * jnp.exp(gc - m_new)
        keep = jnp.exp(b_last + m_prev - m_new)
        c_ref[h] = keep * ch + jnp.dot(kw.T.astype(BF16), vb, preferred_element_type=F32)
        n_ref[h:h + 1, :] = keep * nh + jnp.sum(kw, axis=0, keepdims=True)
        m_ref[:, h:h + 1] = m_new

    @pl.when(c == pl.num_programs(1) - 1)
    def _():
        c1_ref[...] = c_ref[...]
        n1_ref[...] = n_ref[...]
        m1_ref[...] = m_ref[...]


def mlstm_mix(qk_pre, v, o_pre, gates, conv_w, conv_b, gate_b, hist, c0, n0, m0, L, n_valid):
    b, t, qk2 = qk_pre.shape
    _, hh, dk, dv = c0.shape
    vw = hh * dv
    assert t % L == 0
    tril = jnp.tril(jnp.ones((L, L), F32))
    row3 = lambda i, j: (i, j, 0)
    fix3 = lambda i, j: (i, 0, 0)
    fix2 = lambda i, j: (0, 0)
    return pl.pallas_call(
        functools.partial(_mlstm_kernel, L=L, H=hh, Dk=dk, Dv=dv, n_valid=n_valid, kscale=dk ** -0.5),
        grid=(b, t // L),
        in_specs=[pl.BlockSpec((None, L, qk2), row3),
                  pl.BlockSpec((None, L, vw), row3),
                  pl.BlockSpec((None, L, vw), row3),
                  pl.BlockSpec((None, L, LANES), row3),
                  pl.BlockSpec(conv_w.shape, fix2),
                  pl.BlockSpec(conv_b.shape, fix2),
                  pl.BlockSpec(gate_b.shape, fix2),
                  pl.BlockSpec((None, CONV_HALO, qk2), fix3),
                  pl.BlockSpec((None, hh, dk, dv), lambda i, j: (i, 0, 0, 0)),
                  pl.BlockSpec((None, hh, dk), fix3),
                  pl.BlockSpec((None, 1, LANES), fix3),
                  pl.BlockSpec((L, L), fix2),
                  pl.BlockSpec((L, L), fix2)],
        out_specs=[pl.BlockSpec((None, L, vw), row3),
                   pl.BlockSpec((None, hh, dk, dv), lambda i, j: (i, 0, 0, 0)),
                   pl.BlockSpec((None, hh, dk), fix3),
                   pl.BlockSpec((None, 1, LANES), fix3)],
        out_shape=[jax.ShapeDtypeStruct((b, t, vw), BF16),
                   jax.ShapeDtypeStruct(c0.shape, F32),
                   jax.ShapeDtypeStruct(n0.shape, F32),
                   jax.ShapeDtypeStruct(m0.shape, F32)],
        scratch_shapes=[pltpu.VMEM((L + CONV_HALO, qk2), F32),
                        pltpu.VMEM((hh, dk, dv), F32),
                        pltpu.VMEM((hh, dk), F32),
                        pltpu.VMEM((1, LANES), F32)],
        compiler_params=_params(("arbitrary", "arbitrary")),
        name="mlstm_mix",
    )(qk_pre, v, o_pre, gates, conv_w, conv_b, gate_b, hist, c0, n0, m0, tril, tril.T)


def _ln_kernel(*refs, alpha, with_next, with_router, n_experts, next_dtype):
    x_ref, y_ref, gate_ref, lg_ref, lb_ref = refs[:5]
    pos = 5
    if with_next:
        sc_ref, sh_ref = refs[pos:pos + 2]
        pos += 2
    if with_router:
        rw_ref, rb_ref = refs[pos:pos + 2]
        pos += 2
    n_out = 1 + int(with_next) + 2 * int(with_router)
    outs = refs[len(refs) - n_out:]
    xo_ref = outs[0]

    v = alpha * x_ref[...] + (1.0 + gate_ref[...]) * y_ref[...]
    mu = jnp.mean(v, axis=-1, keepdims=True)
    vc = v - mu
    var = jnp.mean(vc * vc, axis=-1, keepdims=True)
    xn = vc * lax.rsqrt(var + LN_EPS) * lg_ref[...] + lb_ref[...]
    xo_ref[...] = xn
    if not with_next:
        return
    h = xn * (1.0 + sc_ref[...]) + sh_ref[...]
    outs[1][...] = h.astype(next_dtype)
    if not with_router:
        return
    tm = h.shape[0]
    logits = jnp.dot(h.astype(BF16), rw_ref[...].astype(BF16), preferred_element_type=F32) + rb_ref[...]
    lane = lax.broadcasted_iota(I32, (tm, LANES), 1).astype(F32)
    logits = jnp.where(lane < n_experts, logits, -jnp.inf)
    idx_out = jnp.zeros((tm, LANES), F32)
    gate_out = jnp.zeros((tm, LANES), F32)
    top0 = None
    for k in range(TOP_K):
        mk = jnp.max(logits, axis=1, keepdims=True)
        ik = jnp.min(jnp.where(logits == mk, lane, float(LANES)), axis=1, keepdims=True)
        logits = jnp.where(lane == ik, -jnp.inf, logits)
        if top0 is None:
            top0 = mk
        idx_out = jnp.where(lane == k, ik, idx_out)
        gate_out = jnp.where(lane == k, jnp.exp(mk - top0), gate_out)
    gate_out = gate_out / jnp.sum(gate_out, axis=1, keepdims=True)
    outs[2][...] = idx_out.astype(I32)
    outs[3][...] = gate_out


def deepnorm(x, y, y_row_off, gate, ln_g, ln_b, alpha, rows_per_group, nxt=None, router=None,
             dest=None, dest_row_off=0, next_dtype=BF16):
    m, d = x.shape
    tm = min(256, rows_per_group)
    assert rows_per_group % tm == 0 and y_row_off % tm == 0 and dest_row_off % tm == 0
    bpg = rows_per_group // tm
    yo = y_row_off // tm
    do = dest_row_off // tm
    row = lambda i: (i, 0)
    grp = lambda i: (i // bpg, 0, 0)
    fix = lambda i: (0, 0)
    args = [x, y, gate, ln_g, ln_b]
    in_specs = [pl.BlockSpec((tm, d), row), pl.BlockSpec((tm, d), lambda i: (i + yo, 0)),
                pl.BlockSpec((None, 1, d), grp), pl.BlockSpec((1, d), fix), pl.BlockSpec((1, d), fix)]
    out_shape = [jax.ShapeDtypeStruct((m, d), F32)]
    out_specs = [pl.BlockSpec((tm, d), row)]
    n_experts = 0
    if nxt is not None:
        args += list(nxt)
        in_specs += [pl.BlockSpec((None, 1, d), grp)] * 2
    if router is not None:
        rw, rb, n_experts = router
        args += [rw, rb]
        in_specs += [pl.BlockSpec(rw.shape, fix), pl.BlockSpec(rb.shape, fix)]
    aliases = {}
    dst = lambda i: (i + do, 0)
    if nxt is not None:
        widths = [(d, next_dtype)] + ([(LANES, I32), (LANES, F32)] if router is not None else [])
        for k, (wd, dt) in enumerate(widths):
            if dest is not None:
                aliases[len(args)] = 1 + k
                args.append(dest[k])
                in_specs.append(pl.BlockSpec(memory_space=pl.ANY))
                out_shape.append(jax.ShapeDtypeStruct(dest[k].shape, dt))
            else:
                out_shape.append(jax.ShapeDtypeStruct((m, wd), dt))
            out_specs.append(pl.BlockSpec((tm, wd), dst))
    return pl.pallas_call(
        functools.partial(_ln_kernel, alpha=alpha, with_next=nxt is not None, with_router=router is not None,
                          n_experts=n_experts, next_dtype=next_dtype),
        grid=(m // tm,),
        in_specs=in_specs,
        out_specs=out_specs,
        out_shape=out_shape,
        input_output_aliases=aliases,
        compiler_params=_params(("arbitrary",)),
        name="deepnorm",
    )(*args)


def _gather_kernel(idx_ref, idx_next_ref, src_ref, o_ref, buf_ref, sem, *, rb):
    i = pl.program_id(0)
    slot = lax.rem(i, 2)

    def copy(sl, r, src_row):
        return pltpu.make_async_copy(src_ref.at[pl.ds(src_row, 1)], buf_ref.at[sl, pl.ds(r, 1)], sem.at[sl])

    def start_block(rows_ref, sl):
        for r in range(rb):
            copy(sl, r, rows_ref[0, r]).start(priority=r % 2)

    @pl.when(i == 0)
    def _():
        start_block(idx_ref, 0)

    @pl.when(i + 1 < pl.num_programs(0))
    def _():
        start_block(idx_next_ref, 1 - slot)

    for r in range(rb):
        copy(slot, r, 0).wait()
    o_ref[...] = buf_ref[slot].astype(o_ref.dtype)


def gather_rows(src, idx, out_dtype):
    p = idx.shape[0]
    d = src.shape[1]
    rb = GATHER_ROWS
    assert p % rb == 0
    nblk = p // rb
    idx3 = idx.reshape(nblk, 1, rb)
    return pl.pallas_call(
        functools.partial(_gather_kernel, rb=rb),
        grid=(nblk,),
        in_specs=[pl.BlockSpec((None, 1, rb), lambda i: (i, 0, 0), memory_space=pltpu.SMEM),
                  pl.BlockSpec((None, 1, rb), lambda i: (jnp.minimum(i + 1, nblk - 1), 0, 0),
                               memory_space=pltpu.SMEM),
                  pl.BlockSpec(memory_space=pl.ANY)],
        out_specs=pl.BlockSpec((rb, d), lambda i: (i, 0)),
        out_shape=jax.ShapeDtypeStruct((p, d), out_dtype),
        scratch_shapes=[pltpu.VMEM((2, rb, d), src.dtype), pltpu.SemaphoreType.DMA((2,))],
        compiler_params=_params(("arbitrary",)),
        name="moe_gather",
    )(idx3, idx3, src)


def _moe_up_kernel(be_ref, nb_ref, x_ref, wg_ref, wl_ref, bg_ref, bl_ref, o_ref, wgb_ref, wlb_ref):
    s = pl.program_id(1)
    first = jnp.logical_or(s == 0, be_ref[s] != be_ref[jnp.maximum(s - 1, 0)])

    @pl.when(first)
    def _():
        wgb_ref[...] = wg_ref[...].astype(BF16)
        wlb_ref[...] = wl_ref[...].astype(BF16)

    @pl.when(s < nb_ref[0])
    def _():
        x = x_ref[...]
        tn = o_ref.shape[1]
        for c0 in range(0, tn, MXU_COLS):
            cols = slice(c0, min(c0 + MXU_COLS, tn))
            hg = jnp.dot(x, wgb_ref[:, cols], preferred_element_type=F32) + bg_ref[:, cols]
            hl = jnp.dot(x, wlb_ref[:, cols], preferred_element_type=F32) + bl_ref[:, cols]
            glu = jnp.minimum(hg, SWIGLU_LIMIT)
            lin = jnp.clip(hl, -SWIGLU_LIMIT, SWIGLU_LIMIT)
            act = glu * jax.nn.sigmoid(SWIGLU_ALPHA * glu) * (lin + 1.0)
            o_ref[:, cols] = act.astype(o_ref.dtype)

    @pl.when(s >= nb_ref[0])
    def _():
        o_ref[...] = jnp.zeros_like(o_ref)


def moe_up(xs, w_up, b_up, layer, block_e, n_blocks_used):
    p, d = xs.shape
    f = w_up.shape[3] // 2
    tn = min(1024, f)
    nj = f // tn
    nb = p // MOE_ROWS
    grid_spec = pltpu.PrefetchScalarGridSpec(
        num_scalar_prefetch=2,
        grid=(nj, nb),
        in_specs=[pl.BlockSpec((MOE_ROWS, d), lambda j, s, be, nu: (s, 0)),
                  pl.BlockSpec((None, None, d, tn), lambda j, s, be, nu: (layer, be[s], 0, j)),
                  pl.BlockSpec((None, None, d, tn), lambda j, s, be, nu: (layer, be[s], 0, nj + j)),
                  pl.BlockSpec((None, None, 1, tn), lambda j, s, be, nu: (layer, be[s], 0, j)),
                  pl.BlockSpec((None, None, 1, tn), lambda j, s, be, nu: (layer, be[s], 0, nj + j))],
        out_specs=pl.BlockSpec((MOE_ROWS, tn), lambda j, s, be, nu: (s, j)),
        scratch_shapes=[pltpu.VMEM((d, tn), BF16), pltpu.VMEM((d, tn), BF16)])
    return pl.pallas_call(
        _moe_up_kernel,
        grid_spec=grid_spec,
        out_shape=jax.ShapeDtypeStruct((p, f), BF16),
        compiler_params=_params(("arbitrary", "arbitrary")),
        name="moe_up",
    )(block_e, n_blocks_used, xs, w_up, w_up, b_up, b_up)


def _moe_down_kernel(be_ref, nb_ref, a_ref, w_ref, b_ref, o_ref, wb_ref):
    s = pl.program_id(1)
    first = jnp.logical_or(s == 0, be_ref[s] != be_ref[jnp.maximum(s - 1, 0)])

    @pl.when(first)
    def _():
        wb_ref[...] = w_ref[...].astype(BF16)

    @pl.when(s < nb_ref[0])
    def _():
        o_ref[...] = jnp.dot(a_ref[...], wb_ref[...], preferred_element_type=F32) + b_ref[...]

    @pl.when(s >= nb_ref[0])
    def _():
        o_ref[...] = jnp.zeros_like(o_ref)


def moe_down(act, w_down, b_down, layer, block_e, n_blocks_used):
    p, f = act.shape
    d = w_down.shape[3]
    nb = p // MOE_ROWS
    tn = min(1024, d)
    grid_spec = pltpu.PrefetchScalarGridSpec(
        num_scalar_prefetch=2,
        grid=(d // tn, nb),
        in_specs=[pl.BlockSpec((MOE_ROWS, f), lambda j, s, be, nu: (s, 0)),
                  pl.BlockSpec((None, None, f, tn), lambda j, s, be, nu: (layer, be[s], 0, j)),
                  pl.BlockSpec((None, None, 1, tn), lambda j, s, be, nu: (layer, be[s], 0, j))],
        out_specs=pl.BlockSpec((MOE_ROWS, tn), lambda j, s, be, nu: (s, j)),
        scratch_shapes=[pltpu.VMEM((f, tn), BF16)])
    return pl.pallas_call(
        _moe_down_kernel,
        grid_spec=grid_spec,
        out_shape=jax.ShapeDtypeStruct((p, d), F32),
        compiler_params=_params(("arbitrary", "arbitrary")),
        name="moe_down",
    )(block_e, n_blocks_used, act, w_down, b_down)


def _combine_kernel(idx_ref, idx_next_ref, gate_ref, rows_ref, o_ref, buf_ref, sem, *, tc):
    i = pl.program_id(0)
    slot = lax.rem(i, 2)

    def copy(sl, k, r, src_row):
        return pltpu.make_async_copy(rows_ref.at[pl.ds(src_row, 1)], buf_ref.at[sl, k, pl.ds(r, 1)], sem.at[sl])

    def start_block(rows_idx_ref, sl):
        for k in range(TOP_K):
            for r in range(tc):
                copy(sl, k, r, rows_idx_ref[0, k * tc + r]).start(priority=r % 2)

    @pl.when(i == 0)
    def _():
        start_block(idx_ref, 0)

    @pl.when(i + 1 < pl.num_programs(0))
    def _():
        start_block(idx_next_ref, 1 - slot)

    for k in range(TOP_K):
        for r in range(tc):
            copy(slot, k, r, 0).wait()
    gate = gate_ref[...]
    acc = buf_ref[slot, 0] * gate[:, 0:1]
    for k in range(1, TOP_K):
        acc = acc + buf_ref[slot, k] * gate[:, k:k + 1]
    o_ref[...] = acc


def moe_combine(rows, dest, gates):
    n = dest.shape[0]
    d = rows.shape[1]
    tc = COMBINE_TOKENS
    assert n % tc == 0
    nblk = n // tc
    idx = dest.reshape(nblk, tc, TOP_K).transpose(0, 2, 1).reshape(nblk, 1, TOP_K * tc)
    return pl.pallas_call(
        functools.partial(_combine_kernel, tc=tc),
        grid=(nblk,),
        in_specs=[pl.BlockSpec((None, 1, TOP_K * tc), lambda i: (i, 0, 0), memory_space=pltpu.SMEM),
                  pl.BlockSpec((None, 1, TOP_K * tc), lambda i: (jnp.minimum(i + 1, nblk - 1), 0, 0),
                               memory_space=pltpu.SMEM),
                  pl.BlockSpec((tc, LANES), lambda i: (i, 0)),
                  pl.BlockSpec(memory_space=pl.ANY)],
        out_specs=pl.BlockSpec((tc, d), lambda i: (i, 0)),
        out_shape=jax.ShapeDtypeStruct((n, d), F32),
        scratch_shapes=[pltpu.VMEM((2, TOP_K, tc, d), F32), pltpu.SemaphoreType.DMA((2,))],
        compiler_params=_params(("arbitrary",)),
        name="moe_combine",
    )(idx, idx, gates, rows)


def moe_ffn(h_all, top_idx, top_gate, w_up, b_up, w_down, b_down, layer):
    n, d = h_all.shape
    n_exp = w_up.shape[1]
    n_pairs = n * TOP_K
    flat_e = top_idx[:, :TOP_K].reshape(-1)
    blk = 256
    assert n_pairs % blk == 0
    onehot = (flat_e[:, None] == jnp.arange(n_exp, dtype=I32)[None, :]).astype(F32).reshape(n_pairs // blk, blk, n_exp)
    within = jnp.einsum('ts,bse->bte', jnp.tril(jnp.ones((blk, blk), F32)), onehot)
    blk_tot = within[:, -1, :]
    blk_end = jnp.cumsum(blk_tot, axis=0)
    csum = within + (blk_end - blk_tot)[:, None, :]
    counts = blk_end[-1].astype(I32)
    rank = jnp.sum((csum - onehot) * onehot, axis=-1).reshape(-1).astype(I32)
    padded = (counts + MOE_ROWS - 1) // MOE_ROWS * MOE_ROWS
    pend = jnp.cumsum(padded)
    pstart = pend - padded
    dest = pstart[flat_e] + rank
    nb = -(-n_pairs // MOE_ROWS) + n_exp
    p = nb * MOE_ROWS
    row_tok = (jnp.arange(p, dtype=I32) % n).at[dest].set(jnp.arange(n_pairs, dtype=I32) // TOP_K,
                                                          unique_indices=True)
    block_start = jnp.arange(nb, dtype=I32) * MOE_ROWS
    block_e = jnp.minimum(jnp.sum((pend[None, :] <= block_start[:, None]).astype(I32), axis=1), n_exp - 1)
    n_used = (pend[-1] // MOE_ROWS).astype(I32).reshape(1)

    xs = gather_rows(h_all, row_tok, BF16)
    act = moe_up(xs, w_up, b_up, layer, block_e, n_used)
    rows = moe_down(act, w_down, b_down, layer, block_e, n_used)
    return moe_combine(rows, dest.reshape(n, TOP_K).astype(I32), top_gate)


def _pad_rows_front(a, rows):
    pad = rows - a.shape[1]
    return jnp.pad(a, ((0, 0), (pad, 0), (0, 0)))


def _pad_lanes(a):
    return jnp.pad(a, [(0, 0)] * (a.ndim - 1) + [(0, LANES - a.shape[-1])])


def kernel(x_prompt, x_sample, c_prompt, c_sample, cache_pool, cache_k, cache_v, cache_conv, state_C, state_n, state_m, ab_w_in, ab_w_pool, ab_pool_scale, ab_w_out, ml_w_in, ml_conv_w, ml_conv_b, ml_b_i, ml_b_f, ml_w_out, ada_w, ada_b, ln_g, ln_b, router_w, router_b, moe_w_up, moe_b_up, moe_w_down, moe_b_down):
    bp, tp, d = x_prompt.shape
    bs, ts, _ = x_sample.shape
    depth = ada_w.shape[0]
    alpha = (2 * depth) ** 0.25
    past, sb_heads, sb_dim = cache_k.shape[2:]
    sb_w = sb_heads * sb_dim
    pw = cache_pool.shape[-1]
    ml_heads, dk, dv = state_C.shape[2:]
    qkw = 2 * ml_heads * dk
    vw = ml_heads * dv
    n_exp = router_w.shape[-1]
    mp, ms = bp * tp, bs * ts
    n_tok = mp + ms

    n_c = bp + bs
    c_rows = -(-n_c // 16) * 16
    c_all = jnp.pad(jnp.concatenate([c_prompt, c_sample], axis=0), ((0, c_rows - n_c), (0, 0)))
    mod = adaln(c_all, ada_w, ada_b).reshape(depth, c_rows, 6, d)

    def terms(layer, which):
        rows = slice(0, bp) if which == 0 else slice(bp, bp + bs)
        return [mod[layer, rows, i, :][:, None, :] for i in range(6)]

    router_wp = _pad_lanes(router_w)
    router_bp = _pad_lanes(router_b)[:, None, :]
    b_up4 = moe_b_up[:, :, None, :]
    b_down4 = moe_b_down[:, :, None, :]

    tq = min(512, tp // 2)
    tk_att = min(256, tq)
    tkn = LANES
    tri_n = max(tk_att, min(512, past), tkn)
    tri = (jnp.arange(tri_n)[:, None] > jnp.arange(tri_n)[None, :]).astype(BF16)

    xs_res = [x_prompt.reshape(mp, d), x_sample.reshape(ms, d)]
    trunk_rows = [tp, ts]
    trunk_b = [bp, bs]
    row_off = [0, mp]
    h_mix = [None, None]
    pools, ks, vs, convs, cs, ns, mstates = [[[], []] for _ in range(7)]

    for layer in range(depth):
        j = layer // 2
        t6 = [terms(layer, 0), terms(layer, 1)]
        if layer == 0:
            for w in range(2):
                sh_m, sc_m = t6[w][0], t6[w][1]
                h_mix[w] = modulate(xs_res[w].reshape(trunk_b[w], trunk_rows[w], d), sc_m, sh_m)
        y_mix = [None, None]
        if layer % 2 == 0:
            w_in = ab_w_in[j].astype(BF16)
            w_out = ab_w_out[j].astype(BF16)
            w_pool = ab_w_pool[j].astype(BF16)
            scale = ab_pool_scale[j][None, :]
            for w in range(2):
                bb, tt = trunk_b[w], trunk_rows[w]
                u, q, k, v = [matmul([h_mix[w]], w_in, [0], c0, wd)
                              for c0, wd in ((0, pw), (pw, sb_w), (pw + sb_w, sb_w), (pw + 2 * sb_w, sb_w))]
                u3 = u.reshape(bb, tt, pw)
                if w == 0:
                    hist = jnp.zeros((bb, POOL_HALO, pw), F32)
                    pool_out = pool_mix(u3, hist, w_pool, scale, 0)
                    att = attn_prompt(q.reshape(bb, tt, sb_w), k.reshape(bb, tt, sb_w), v.reshape(bb, tt, sb_w),
                                      tri[:tk_att, :tk_att], sb_heads, tq)
                    pools[w].append(u3[:, tt - (POOL_HALO - 1):])
                else:
                    hist = _pad_rows_front(cache_pool[j], POOL_HALO)
                    pool_out = pool_mix(u3, hist, w_pool, scale, past)
                    z_pool = jnp.concatenate([cache_pool[j], u3], axis=1)
                    pools[w].append(z_pool[:, -(POOL_HALO - 1):])
                    q4 = q.reshape(bb, tt, sb_heads, sb_dim)
                    eye = jnp.eye(sb_heads, dtype=F32)
                    q_bd = jnp.einsum('bihd,hg->bhigd', q4, eye).reshape(bb, sb_heads * tt, sb_w).astype(BF16)
                    k_new = jnp.pad(k.reshape(bb, tt, sb_w), ((0, 0), (0, tkn - tt), (0, 0)))
                    v_new = jnp.pad(v.reshape(bb, tt, sb_w), ((0, 0), (0, tkn - tt), (0, 0)))
                    o_bd = attn_sample(q_bd, k_new, v_new, cache_k[j].reshape(bb, past * sb_heads, sb_dim),
                                       cache_v[j].reshape(bb, past * sb_heads, sb_dim), tri, tt, sb_dim)
                    o5 = o_bd.reshape(bb, sb_heads, tt, sb_heads, sb_dim)
                    att = jnp.einsum('bhihd->bihd', o5).reshape(bb * tt, sb_w).astype(BF16)
                ks[w].append(k.reshape(bb, tt, sb_heads, sb_dim))
                vs[w].append(v.reshape(bb, tt, sb_heads, sb_dim))
                y_mix[w] = matmul([pool_out, att], w_out, [0, pw], 0, d)
        else:
            w_in = ml_w_in[j]
            o1 = qkw
            o2 = o1 + vw
            o3 = o2 + vw
            w_main = w_in[:, :o3].astype(BF16)
            w_gate = _pad_lanes(w_in[:, o3:]).astype(BF16)
            w_out = ml_w_out[j].astype(BF16)
            conv_w = jnp.pad(ml_conv_w[j], ((0, CONV_HALO - CONV_W), (0, 0)))
            conv_b = ml_conv_b[j][None, :]
            gate_b = _pad_lanes(jnp.concatenate([ml_b_i[j], ml_b_f[j]]))[None, :]
            for w in range(2):
                bb, tt = trunk_b[w], trunk_rows[w]
                qk_pre = matmul([h_mix[w]], w_main, [0], 0, qkw).reshape(bb, tt, qkw)
                v_in = matmul([h_mix[w]], w_main, [0], o1, vw).reshape(bb, tt, vw)
                o_pre = matmul([h_mix[w]], w_main, [0], o2, vw).reshape(bb, tt, vw)
                gates = matmul([h_mix[w]], w_gate, [0], 0, LANES, tn=LANES).reshape(bb, tt, LANES)
                if w == 0:
                    hist = jnp.zeros((bb, CONV_HALO, qkw), F32)
                    c0 = jnp.zeros((bb, ml_heads, dk, dv), F32)
                    n0 = jnp.zeros((bb, ml_heads, dk), F32)
                    m0 = jnp.zeros((bb, 1, LANES), F32)
                    chunk = min(256, tt)
                    hid, c1, n1, m1 = mlstm_mix(qk_pre, v_in, o_pre, gates, conv_w, conv_b, gate_b, hist,
                                                c0, n0, m0, chunk, chunk)
                    convs[w].append(qk_pre[:, tt - (CONV_W - 1):])
                else:
                    hist = _pad_rows_front(cache_conv[j], CONV_HALO)
                    m0 = _pad_lanes(state_m[j])[:, None, :]
                    chunk = -(-tt // LANES) * LANES
                    padt = lambda a: jnp.pad(a, ((0, 0), (0, chunk - tt), (0, 0)))
                    hid, c1, n1, m1 = mlstm_mix(padt(qk_pre), padt(v_in), padt(o_pre), padt(gates), conv_w, conv_b,
                                                gate_b, hist, state_C[j], state_n[j], m0, chunk, tt)
                    hid = hid[:, :tt]
                    z_conv = jnp.concatenate([cache_conv[j], qk_pre], axis=1)
                    convs[w].append(z_conv[:, -(CONV_W - 1):])
                cs[w].append(c1)
                ns[w].append(n1)
                mstates[w].append(m1[:, 0, :ml_heads])
                y_mix[w] = matmul([hid.reshape(bb * tt, vw)], w_out, [0], 0, d)

        lg1, lb1 = ln_g[layer, 0][None, :], ln_b[layer, 0][None, :]
        lg2, lb2 = ln_g[layer, 1][None, :], ln_b[layer, 1][None, :]
        router = (router_wp[layer], router_bp[layer], n_exp)
        shared = (jnp.zeros((n_tok, d), F32), jnp.zeros((n_tok, LANES), I32), jnp.zeros((n_tok, LANES), F32))
        x_mid = [None, None]
        for w in range(2):
            g_m, sh_f, sc_f = t6[w][2], t6[w][3], t6[w][4]
            x_mid[w], *shared = deepnorm(xs_res[w], y_mix[w], 0, g_m, lg1, lb1, alpha, trunk_rows[w],
                                         nxt=(sc_f, sh_f), router=router, dest=tuple(shared),
                                         dest_row_off=row_off[w], next_dtype=F32)
        h_all, top_idx, top_gate = shared
        y_ffn = moe_ffn(h_all, top_idx, top_gate, moe_w_up, b_up4, moe_w_down, b_down4, layer)
        for w in range(2):
            g_f = t6[w][5]
            if layer + 1 < depth:
                t_next = terms(layer + 1, w)
                xs_res[w], h_mix[w] = deepnorm(x_mid[w], y_ffn, row_off[w], g_f, lg2, lb2, alpha, trunk_rows[w],
                                               nxt=(t_next[1], t_next[0]))
            else:
                (xs_res[w],) = deepnorm(x_mid[w], y_ffn, row_off[w], g_f, lg2, lb2, alpha, trunk_rows[w])

    outs = []
    for w in range(2):
        outs.append((xs_res[w].reshape(trunk_b[w], trunk_rows[w], d), jnp.stack(pools[w]), jnp.stack(ks[w]),
                     jnp.stack(vs[w]), jnp.stack(convs[w]), jnp.stack(cs[w]), jnp.stack(ns[w]),
                     jnp.stack(mstates[w])))
    (y_p, *rest_p), (y_s, *rest_s) = outs
    return (y_p, y_s, *rest_p, *rest_s)
```

```python
import functools

import jax
import jax.numpy as jnp
from jax import lax
from jax.experimental import pallas as pl
from jax.experimental.pallas import tpu as pltpu

F32 = jnp.float32
BF16 = jnp.bfloat16
I32 = jnp.int32

TOP_K = 4
POOL_WINDOWS = (2, 4, 8, 16)
POOL_HALO = 16
CONV_W = 4
CONV_HALO = 8
SWIGLU_LIMIT = 7.0
SWIGLU_ALPHA = 1.702
LN_EPS = 1e-5
LANES = 128
MXU_COLS = 256
MOE_ROWS = 256
GATHER_ROWS = 256
COMBINE_TOKENS = 64
VMEM_LIMIT = 56 * 1024 * 1024
NEG_BIG = -1e30

_HIGHEST = lax.Precision.HIGHEST


def _params(sem, vmem=None):
    return pltpu.CompilerParams(dimension_semantics=sem, vmem_limit_bytes=vmem or VMEM_LIMIT)


def _log_sigmoid(z):
    return jnp.minimum(z, 0.0) - jnp.log(1.0 + jnp.exp(-jnp.abs(z)))


def _adaln_kernel(c_ref, w_ref, b_ref, o_ref):
    c = c_ref[...]
    s = (c * jax.nn.sigmoid(c)).astype(BF16)
    o_ref[...] = jnp.dot(s, w_ref[...].astype(BF16), preferred_element_type=F32) + b_ref[...]


def adaln(c_all, ada_w, ada_b):
    nl, d, n = ada_w.shape
    c = c_all.shape[0]
    tn = min(1024, n)
    return pl.pallas_call(
        _adaln_kernel,
        grid=(nl, n // tn),
        in_specs=[pl.BlockSpec((c, d), lambda l, j: (0, 0)),
                  pl.BlockSpec((None, d, tn), lambda l, j: (l, 0, j)),
                  pl.BlockSpec((None, 1, tn), lambda l, j: (l, 0, j))],
        out_specs=pl.BlockSpec((None, c, tn), lambda l, j: (l, 0, j)),
        out_shape=jax.ShapeDtypeStruct((nl, c, n), F32),
        compiler_params=_params(("arbitrary", "arbitrary")),
        name="adaln",
    )(c_all, ada_w, ada_b.reshape(nl, 1, n))


def _modulate_kernel(x_ref, sc_ref, sh_ref, o_ref):
    o_ref[...] = (x_ref[...] * (1.0 + sc_ref[...]) + sh_ref[...]).astype(o_ref.dtype)


def modulate(x, sc, sh):
    g, r, d = x.shape
    tr = min(512, r)
    out = pl.pallas_call(
        _modulate_kernel,
        grid=(g, r // tr),
        in_specs=[pl.BlockSpec((None, tr, d), lambda b, i: (b, i, 0)),
                  pl.BlockSpec((None, 1, d), lambda b, i: (b, 0, 0)),
                  pl.BlockSpec((None, 1, d), lambda b, i: (b, 0, 0))],
        out_specs=pl.BlockSpec((None, tr, d), lambda b, i: (b, i, 0)),
        out_shape=jax.ShapeDtypeStruct((g, r, d), BF16),
        compiler_params=_params(("arbitrary", "arbitrary")),
        name="modulate",
    )(x, sc, sh)
    return out.reshape(g * r, d)


def _matmul_kernel(*refs, n_a):
    a_refs, w_refs, o_ref = refs[:n_a], refs[n_a:2 * n_a], refs[2 * n_a]
    acc = jnp.dot(a_refs[0][...], w_refs[0][...], preferred_element_type=F32)
    for a, w in zip(a_refs[1:], w_refs[1:]):
        acc = acc + jnp.dot(a[...], w[...], preferred_element_type=F32)
    o_ref[...] = acc.astype(o_ref.dtype)


def matmul(a_list, w, row_offs, col_off, n_cols, out_dtype=F32, tm=512, tn=1024):
    m = a_list[0].shape[0]
    tm = min(tm, m)
    tn = min(tn, n_cols)
    assert m % tm == 0 and n_cols % tn == 0 and col_off % tn == 0
    n_a = len(a_list)
    in_specs = [pl.BlockSpec((tm, a.shape[1]), lambda j, i: (i, 0)) for a in a_list]
    for a, ro in zip(a_list, row_offs):
        ka = a.shape[1]
        assert ro % ka == 0
        in_specs.append(pl.BlockSpec((ka, tn), functools.partial(
            lambda j, i, rb, cb: (rb, cb + j), rb=ro // ka, cb=col_off // tn)))
    return pl.pallas_call(
        functools.partial(_matmul_kernel, n_a=n_a),
        grid=(n_cols // tn, m // tm),
        in_specs=in_specs,
        out_specs=pl.BlockSpec((tm, tn), lambda j, i: (i, j)),
        out_shape=jax.ShapeDtypeStruct((m, n_cols), out_dtype),
        compiler_params=_params(("arbitrary", "arbitrary")),
        name="matmul",
    )(*a_list, *([w] * n_a))


def _pool_kernel(u_ref, hist_ref, w_ref, scale_ref, o_ref, z_ref, *, tt, pg, pos0):
    ti = pl.program_id(1)

    @pl.when(ti == 0)
    def _():
        z_ref[0:POOL_HALO, :] = hist_ref[...]

    @pl.when(ti > 0)
    def _():
        z_ref[0:POOL_HALO, :] = z_ref[tt:tt + POOL_HALO, :]

    z_ref[POOL_HALO:POOL_HALO + tt, :] = u_ref[...]
    pos = pos0 + ti * tt + lax.broadcasted_iota(I32, (tt, 1), 0)
    for g, w in enumerate(POOL_WINDOWS):
        cols = slice(g * pg, (g + 1) * pg)
        u_g = z_ref[POOL_HALO:POOL_HALO + tt, cols]
        acc = u_g
        for j in range(1, w):
            acc = acc + z_ref[POOL_HALO - j:POOL_HALO - j + tt, cols]
        cnt = jnp.minimum(w, pos + 1).astype(F32)
        diff = (acc / cnt - u_g).astype(BF16)
        y = jnp.dot(diff, w_ref[g], preferred_element_type=F32) * scale_ref[:, cols]
        o_ref[:, cols] = y.astype(o_ref.dtype)


def pool_mix(u, hist, w_pool, scale, pos0):
    b, t, pw = u.shape
    ng, pg, _ = w_pool.shape
    tt = min(512, t)
    assert tt >= POOL_HALO and t % tt == 0 and ng == len(POOL_WINDOWS)
    out = pl.pallas_call(
        functools.partial(_pool_kernel, tt=tt, pg=pg, pos0=pos0),
        grid=(b, t // tt),
        in_specs=[pl.BlockSpec((None, tt, pw), lambda i, j: (i, j, 0)),
                  pl.BlockSpec((None, POOL_HALO, pw), lambda i, j: (i, 0, 0)),
                  pl.BlockSpec((ng, pg, pg), lambda i, j: (0, 0, 0)),
                  pl.BlockSpec((1, pw), lambda i, j: (0, 0))],
        out_specs=pl.BlockSpec((None, tt, pw), lambda i, j: (i, j, 0)),
        out_shape=jax.ShapeDtypeStruct((b, t, pw), BF16),
        scratch_shapes=[pltpu.VMEM((tt + POOL_HALO, pw), F32)],
        compiler_params=_params(("arbitrary", "arbitrary")),
        name="pool_mix",
    )(u, hist, w_pool, scale)
    return out.reshape(b * t, pw)


def _sb_block(z, lsz_mask, tri, v_bf, r_prev):
    lsz = _log_sigmoid(z)
    lk = lsz - z
    if lsz_mask is not None:
        lk = jnp.where(lsz_mask, lk, 0.0)
    hi = lk.astype(BF16)
    lo = (lk - hi.astype(F32)).astype(BF16)
    after = (jnp.dot(hi, tri, preferred_element_type=F32)
             + jnp.dot(lo, tri, preferred_element_type=F32))
    a = jnp.exp(lsz + after + r_prev)
    if lsz_mask is not None:
        a = jnp.where(lsz_mask, a, 0.0)
    pv = jnp.dot(a.astype(BF16), v_bf, preferred_element_type=F32)
    return pv, jnp.sum(lk, axis=1, keepdims=True)


def _attn_prompt_kernel(q_ref, k_ref, v_ref, tri_ref, o_ref, kb_ref, vb_ref, r_ref, acc_ref, *, scale, tq, tk):
    qi = pl.program_id(2)

    @pl.when(qi == 0)
    def _():
        kb_ref[...] = k_ref[...].astype(BF16)
        vb_ref[...] = v_ref[...].astype(BF16)

    q = q_ref[...].astype(BF16)
    r_ref[...] = jnp.zeros_like(r_ref)
    acc_ref[...] = jnp.zeros_like(acc_ref)

    def key_block(k0, masked):
        for sub in reversed(range(tq // tk)):
            ks = pl.multiple_of(k0 + sub * tk, tk)
            z = lax.dot_general(q, kb_ref[pl.ds(ks, tk), :], (((1,), (1,)), ((), ())),
                                preferred_element_type=F32) * scale
            mask = None
            if masked:
                row = lax.broadcasted_iota(I32, (tq, tk), 0)
                col = lax.broadcasted_iota(I32, (tq, tk), 1) + sub * tk
                mask = col < row
            pv, lk_sum = _sb_block(z, mask, tri_ref[...], vb_ref[pl.ds(ks, tk), :], r_ref[...])
            acc_ref[...] += pv
            r_ref[...] += lk_sum

    key_block(qi * tq, True)

    def body(i, carry):
        key_block((qi - 1 - i) * tq, False)
        return carry

    lax.fori_loop(0, qi, body, 0)
    o_ref[...] = acc_ref[...].astype(o_ref.dtype)


def attn_prompt(q, k, v, tri, heads, tq):
    b, t, w = q.shape
    d = w // heads
    tk = tri.shape[0]
    assert t % tq == 0 and tq % tk == 0
    kv_spec = pl.BlockSpec((None, t, d), lambda bi, h, qi: (bi, 0, h))
    out = pl.pallas_call(
        functools.partial(_attn_prompt_kernel, scale=d ** -0.5, tq=tq, tk=tk),
        grid=(b, heads, t // tq),
        in_specs=[pl.BlockSpec((None, tq, d), lambda bi, h, qi: (bi, qi, h)),
                  kv_spec, kv_spec,
                  pl.BlockSpec((tk, tk), lambda bi, h, qi: (0, 0))],
        out_specs=pl.BlockSpec((None, tq, d), lambda bi, h, qi: (bi, qi, h)),
        out_shape=jax.ShapeDtypeStruct((b, t, w), BF16),
        scratch_shapes=[pltpu.VMEM((t, d), BF16), pltpu.VMEM((t, d), BF16),
                        pltpu.VMEM((tq, 1), F32), pltpu.VMEM((tq, d), F32)],
        compiler_params=_params(("arbitrary",) * 3),
        name="attn_prompt",
    )(q, k, v, tri)
    return out.reshape(b * t, w)


def _attn_sample_kernel(q_ref, kn_ref, vn_ref, kp_ref, vp_ref, tri_ref, o_ref, r_ref, *, scale, ts, tkn, tkp, heads):
    j = pl.program_id(1)
    rows = q_ref.shape[0]

    def block(k, v, tk, masked):
        z = lax.dot_general(q_ref[...], k, (((1,), (1,)), ((), ())), preferred_element_type=F32) * scale
        mask = None
        if masked:
            row = lax.rem(lax.broadcasted_iota(I32, (rows, tk), 0), ts)
            col = lax.broadcasted_iota(I32, (rows, tk), 1)
            mask = col < row
        pv, lk_sum = _sb_block(z, mask, tri_ref[0:tk, 0:tk], v, r_ref[...])
        o_ref[...] += pv
        r_ref[...] += lk_sum

    def heads_on_lanes(ref):
        return jnp.concatenate([ref[pl.ds(h, tkp, stride=heads), :].astype(BF16) for h in range(heads)], axis=1)

    @pl.when(j == 0)
    def _():
        r_ref[...] = jnp.zeros_like(r_ref)
        o_ref[...] = jnp.zeros_like(o_ref)
        block(kn_ref[...].astype(BF16), vn_ref[...].astype(BF16), tkn, True)

    @pl.when(j > 0)
    def _():
        block(heads_on_lanes(kp_ref), heads_on_lanes(vp_ref), tkp, False)


def attn_sample(q_bd, k_new, v_new, k_past, v_past, tri, ts, d):
    b, rows, w = q_bd.shape
    heads = w // d
    tkn = k_new.shape[1]
    past = k_past.shape[1] // heads
    tkp = min(512, past)
    assert past % tkp == 0 and tri.shape[0] >= max(tkn, tkp)
    n_past = past // tkp
    past_spec = pl.BlockSpec((None, tkp * heads, d), lambda bi, j: (bi, n_past - jnp.maximum(j, 1), 0))
    new_spec = pl.BlockSpec((None, tkn, w), lambda bi, j: (bi, 0, 0))
    return pl.pallas_call(
        functools.partial(_attn_sample_kernel, scale=d ** -0.5, ts=ts, tkn=tkn, tkp=tkp, heads=heads),
        grid=(b, 1 + n_past),
        in_specs=[pl.BlockSpec((None, rows, w), lambda bi, j: (bi, 0, 0)),
                  new_spec, new_spec, past_spec, past_spec,
                  pl.BlockSpec(tri.shape, lambda bi, j: (0, 0))],
        out_specs=pl.BlockSpec((None, rows, w), lambda bi, j: (bi, 0, 0)),
        out_shape=jax.ShapeDtypeStruct((b, rows, w), F32),
        scratch_shapes=[pltpu.VMEM((rows, 1), F32)],
        compiler_params=_params(("arbitrary", "arbitrary")),
        name="attn_sample",
    )(q_bd, k_new, v_new, k_past, v_past, tri)


def _mlstm_kernel(qk_ref, v_ref, op_ref, g_ref, cw_ref, cb_ref, gb_ref, hist_ref, c0_ref, n0_ref, m0_ref,
                  tril_ref, triu_ref,
                  hid_ref, c1_ref, n1_ref, m1_ref,
                  z_ref, c_ref, n_ref, m_ref, *, L, H, Dk, Dv, n_valid, kscale):
    c = pl.program_id(1)
    qw = H * Dk

    @pl.when(c == 0)
    def _():
        z_ref[0:CONV_HALO, :] = hist_ref[...]
        c_ref[...] = c0_ref[...]
        n_ref[...] = n0_ref[...]
        m_ref[...] = m0_ref[...]

    @pl.when(c > 0)
    def _():
        z_ref[0:CONV_HALO, :] = z_ref[L:L + CONV_HALO, :]

    z_ref[CONV_HALO:CONV_HALO + L, :] = qk_ref[...]
    y = cb_ref[...]
    for j in range(CONV_W):
        r0 = CONV_HALO - (CONV_W - 1) + j
        y = y + z_ref[r0:r0 + L, :] * cw_ref[j:j + 1, :]
    qk = y * jax.nn.sigmoid(y)

    g = g_ref[...] + gb_ref[...]
    lane = lax.broadcasted_iota(I32, (L, LANES), 1)
    gl = jnp.where(jnp.logical_and(lane >= H, lane < 2 * H), _log_sigmoid(g), g)
    if n_valid < L:
        row = lax.broadcasted_iota(I32, (L, LANES), 0)
        gl = jnp.where(row < n_valid, gl, jnp.where(lane < H, NEG_BIG, 0.0))
    glt = gl.T
    b_col = jnp.dot(tril_ref[...], gl, preferred_element_type=F32, precision=_HIGHEST)
    b_row = jnp.dot(glt, triu_ref[...], preferred_element_type=F32, precision=_HIGHEST)
    trow = lax.broadcasted_iota(I32, (L, L), 0)
    tcol = lax.broadcasted_iota(I32, (L, L), 1)
    causal = tcol <= trow

    for h in range(H):
        m_prev = m_ref[:, h:h + 1]
        bc = b_col[:, H + h:H + h + 1]
        br = b_row[H + h:H + h + 1, :]
        igc = gl[:, h:h + 1]
        igr = glt[h:h + 1, :]
        dm = jnp.where(causal, bc - br + igr, -jnp.inf)
        inter = bc + m_prev
        m_t = jnp.maximum(inter, jnp.max(dm, axis=1, keepdims=True))
        qh = qk[:, h * Dk:(h + 1) * Dk]
        kh = qk[:, qw + h * Dk:qw + (h + 1) * Dk] * kscale
        qb = qh.astype(BF16)
        vb = v_ref[:, h * Dv:(h + 1) * Dv].astype(BF16)
        s = jnp.exp(dm - m_t) * lax.dot_general(qb, kh.astype(BF16), (((1,), (1,)), ((), ())),
                                                 preferred_element_type=F32)
        decay = jnp.exp(inter - m_t)
        ch = c_ref[h]
        nh = n_ref[h:h + 1, :]
        num = (jnp.dot(s.astype(BF16), vb, preferred_element_type=F32)
               + decay * jnp.dot(qb, ch.astype(BF16), preferred_element_type=F32))
        den = jnp.sum(s, axis=1, keepdims=True) + decay * jnp.sum(qh * nh, axis=1, keepdims=True)
        hval = num / jnp.maximum(jnp.abs(den), jnp.exp(-m_t))
        og = jax.nn.sigmoid(op_ref[:, h * Dv:(h + 1) * Dv])
        hid_ref[:, h * Dv:(h + 1) * Dv] = (og * hval).astype(hid_ref.dtype)

        b_last = bc[L - 1:L, :]
        gc = b_last - bc + igc
        m_new = jnp.maximum(b_last + m_prev, jnp.max(gc, axis=0, keepdims=True))
        kw = kh * jnp.exp(gc - m_new)
        keep = jnp.exp(b_last + m_prev - m_new)
        c_ref[h] = keep * ch + jnp.dot(kw.T.astype(BF16), vb, preferred_element_type=F32)
        n_ref[h:h + 1, :] = keep * nh + jnp.sum(kw, axis=0, keepdims=True)
        m_ref[:, h:h + 1] = m_new

    @pl.when(c == pl.num_programs(1) - 1)
    def _():
        c1_ref[...] = c_ref[...]
        n1_ref[...] = n_ref[...]
        m1_ref[...] = m_ref[...]


def mlstm_mix(qk_pre, v, o_pre, gates, conv_w, conv_b, gate_b, hist, c0, n0, m0, L, n_valid):
    b, t, qk2 = qk_pre.shape
    _, hh, dk, dv = c0.shape
    vw = hh * dv
    assert t % L == 0
    tril = jnp.tril(jnp.ones((L, L), F32))
    row3 = lambda i, j: (i, j, 0)
    fix3 = lambda i, j: (i, 0, 0)
    fix2 = lambda i, j: (0, 0)
    return pl.pallas_call(
        functools.partial(_mlstm_kernel, L=L, H=hh, Dk=dk, Dv=dv, n_valid=n_valid, kscale=dk ** -0.5),
        grid=(b, t // L),
        in_specs=[pl.BlockSpec((None, L, qk2), row3),
                  pl.BlockSpec((None, L, vw), row3),
                  pl.BlockSpec((None, L, vw), row3),
                  pl.BlockSpec((None, L, LANES), row3),
                  pl.BlockSpec(conv_w.shape, fix2),
                  pl.BlockSpec(conv_b.shape, fix2),
                  pl.BlockSpec(gate_b.shape, fix2),
                  pl.BlockSpec((None, CONV_HALO, qk2), fix3),
                  pl.BlockSpec((None, hh, dk, dv), lambda i, j: (i, 0, 0, 0)),
                  pl.BlockSpec((None, hh, dk), fix3),
                  pl.BlockSpec((None, 1, LANES), fix3),
                  pl.BlockSpec((L, L), fix2),
                  pl.BlockSpec((L, L), fix2)],
        out_specs=[pl.BlockSpec((None, L, vw), row3),
                   pl.BlockSpec((None, hh, dk, dv), lambda i, j: (i, 0, 0, 0)),
                   pl.BlockSpec((None, hh, dk), fix3),
                   pl.BlockSpec((None, 1, LANES), fix3)],
        out_shape=[jax.ShapeDtypeStruct((b, t, vw), BF16),
                   jax.ShapeDtypeStruct(c0.shape, F32),
                   jax.ShapeDtypeStruct(n0.shape, F32),
                   jax.ShapeDtypeStruct(m0.shape, F32)],
        scratch_shapes=[pltpu.VMEM((L + CONV_HALO, qk2), F32),
                        pltpu.VMEM((hh, dk, dv), F32),
                        pltpu.VMEM((hh, dk), F32),
                        pltpu.VMEM((1, LANES), F32)],
        compiler_params=_params(("arbitrary", "arbitrary")),
        name="mlstm_mix",
    )(qk_pre, v, o_pre, gates, conv_w, conv_b, gate_b, hist, c0, n0, m0, tril, tril.T)


def _ln_kernel(*refs, alpha, with_next, with_router, n_experts, next_dtype):
    x_ref, y_ref, gate_ref, lg_ref, lb_ref = refs[:5]
    pos = 5
    if with_next:
        sc_ref, sh_ref = refs[pos:pos + 2]
        pos += 2
    if with_router:
        rw_ref, rb_ref = refs[pos:pos + 2]
        pos += 2
    n_out = 1 + int(with_next) + 2 * int(with_router)
    outs = refs[len(refs) - n_out:]
    xo_ref = outs[0]

    v = alpha * x_ref[...] + (1.0 + gate_ref[...]) * y_ref[...]
    mu = jnp.mean(v, axis=-1, keepdims=True)
    vc = v - mu
    var = jnp.mean(vc * vc, axis=-1, keepdims=True)
    xn = vc * lax.rsqrt(var + LN_EPS) * lg_ref[...] + lb_ref[...]
    xo_ref[...] = xn
    if not with_next:
        return
    h = xn * (1.0 + sc_ref[...]) + sh_ref[...]
    outs[1][...] = h.astype(next_dtype)
    if not with_router:
        return
    tm = h.shape[0]
    logits = jnp.dot(h.astype(BF16), rw_ref[...].astype(BF16), preferred_element_type=F32) + rb_ref[...]
    lane = lax.broadcasted_iota(I32, (tm, LANES), 1).astype(F32)
    logits = jnp.where(lane < n_experts, logits, -jnp.inf)
    idx_out = jnp.zeros((tm, LANES), F32)
    gate_out = jnp.zeros((tm, LANES), F32)
    top0 = None
    for k in range(TOP_K):
        mk = jnp.max(logits, axis=1, keepdims=True)
        ik = jnp.min(jnp.where(logits == mk, lane, float(LANES)), axis=1, keepdims=True)
        logits = jnp.where(lane == ik, -jnp.inf, logits)
        if top0 is None:
            top0 = mk
        idx_out = jnp.where(lane == k, ik, idx_out)
        gate_out = jnp.where(lane == k, jnp.exp(mk - top0), gate_out)
    gate_out = gate_out / jnp.sum(gate_out, axis=1, keepdims=True)
    outs[2][...] = idx_out.astype(I32)
    outs[3][...] = gate_out


def deepnorm(x, y, y_row_off, gate, ln_g, ln_b, alpha, rows_per_group, nxt=None, router=None,
             dest=None, dest_row_off=0, next_dtype=BF16):
    m, d = x.shape
    tm = min(256, rows_per_group)
    assert rows_per_group % tm == 0 and y_row_off % tm == 0 and dest_row_off % tm == 0
    bpg = rows_per_group // tm
    yo = y_row_off // tm
    do = dest_row_off // tm
    row = lambda i: (i, 0)
    grp = lambda i: (i // bpg, 0, 0)
    fix = lambda i: (0, 0)
    args = [x, y, gate, ln_g, ln_b]
    in_specs = [pl.BlockSpec((tm, d), row), pl.BlockSpec((tm, d), lambda i: (i + yo, 0)),
                pl.BlockSpec((None, 1, d), grp), pl.BlockSpec((1, d), fix), pl.BlockSpec((1, d), fix)]
    out_shape = [jax.ShapeDtypeStruct((m, d), F32)]
    out_specs = [pl.BlockSpec((tm, d), row)]
    n_experts = 0
    if nxt is not None:
        args += list(nxt)
        in_specs += [pl.BlockSpec((None, 1, d), grp)] * 2
    if router is not None:
        rw, rb, n_experts = router
        args += [rw, rb]
        in_specs += [pl.BlockSpec(rw.shape, fix), pl.BlockSpec(rb.shape, fix)]
    aliases = {}
    dst = lambda i: (i + do, 0)
    if nxt is not None:
        widths = [(d, next_dtype)] + ([(LANES, I32), (LANES, F32)] if router is not None else [])
        for k, (wd, dt) in enumerate(widths):
            if dest is not None:
                aliases[len(args)] = 1 + k
                args.append(dest[k])
                in_specs.append(pl.BlockSpec(memory_space=pl.ANY))
                out_shape.append(jax.ShapeDtypeStruct(dest[k].shape, dt))
            else:
                out_shape.append(jax.ShapeDtypeStruct((m, wd), dt))
            out_specs.append(pl.BlockSpec((tm, wd), dst))
    return pl.pallas_call(
        functools.partial(_ln_kernel, alpha=alpha, with_next=nxt is not None, with_router=router is not None,
                          n_experts=n_experts, next_dtype=next_dtype),
        grid=(m // tm,),
        in_specs=in_specs,
        out_specs=out_specs,
        out_shape=out_shape,
        input_output_aliases=aliases,
        compiler_params=_params(("arbitrary",)),
        name="deepnorm",
    )(*args)


def _gather_kernel(idx_ref, idx_next_ref, src_ref, o_ref, buf_ref, sem, *, rb):
    i = pl.program_id(0)
    slot = lax.rem(i, 2)

    def copy(sl, r, src_row):
        return pltpu.make_async_copy(src_ref.at[pl.ds(src_row, 1)], buf_ref.at[sl, pl.ds(r, 1)], sem.at[sl])

    def start_block(rows_ref, sl):
        for r in range(rb):
            copy(sl, r, rows_ref[0, r]).start(priority=r % 2)

    @pl.when(i == 0)
    def _():
        start_block(idx_ref, 0)

    @pl.when(i + 1 < pl.num_programs(0))
    def _():
        start_block(idx_next_ref, 1 - slot)

    for r in range(rb):
        copy(slot, r, 0).wait()
    o_ref[...] = buf_ref[slot].astype(o_ref.dtype)


def gather_rows(src, idx, out_dtype):
    p = idx.shape[0]
    d = src.shape[1]
    rb = GATHER_ROWS
    assert p % rb == 0
    nblk = p // rb
    idx3 = idx.reshape(nblk, 1, rb)
    return pl.pallas_call(
        functools.partial(_gather_kernel, rb=rb),
        grid=(nblk,),
        in_specs=[pl.BlockSpec((None, 1, rb), lambda i: (i, 0, 0), memory_space=pltpu.SMEM),
                  pl.BlockSpec((None, 1, rb), lambda i: (jnp.minimum(i + 1, nblk - 1), 0, 0),
                               memory_space=pltpu.SMEM),
                  pl.BlockSpec(memory_space=pl.ANY)],
        out_specs=pl.BlockSpec((rb, d), lambda i: (i, 0)),
        out_shape=jax.ShapeDtypeStruct((p, d), out_dtype),
        scratch_shapes=[pltpu.VMEM((2, rb, d), src.dtype), pltpu.SemaphoreType.DMA((2,))],
        compiler_params=_params(("arbitrary",)),
        name="moe_gather",
    )(idx3, idx3, src)


def _expert_chunks(n, in_copy, out_copy, compute):
    @pl.when(n > 0)
    def _():
        in_copy(0, 0).start()

    def body(i, carry):
        slot = lax.rem(i, 2)
        in_copy(i, slot).wait()

        @pl.when(i + 1 < n)
        def _():
            in_copy(i + 1, 1 - slot).start()

        @pl.when(i >= 2)
        def _():
            out_copy(i - 2, slot).wait()

        compute(slot)
        out_copy(i, slot).start()
        return carry

    lax.fori_loop(0, n, body, 0)

    @pl.when(n >= 2)
    def _():
        out_copy(n - 2, lax.rem(n, 2)).wait()

    @pl.when(n >= 1)
    def _():
        out_copy(n - 1, lax.rem(n + 1, 2)).wait()


def _zero_tail(tail_ref, buf, out_copy_abs):
    buf[0] = jnp.zeros(buf.shape[1:], buf.dtype)
    t0, nt = tail_ref[0], tail_ref[1]

    def start(i, carry):
        out_copy_abs(t0 + i, 0).start()
        return carry

    def wait(i, carry):
        out_copy_abs(t0 + i, 0).wait()
        return carry

    lax.fori_loop(0, nt, start, 0)
    lax.fori_loop(0, nt, wait, 0)


def _moe_up_kernel(cs_ref, nc_ref, tail_ref, x_hbm, wg_ref, wl_ref, bg_ref, bl_ref, act_hbm,
                   wgb_ref, wlb_ref, xbuf, obuf, in_sem, out_sem):
    j = pl.program_id(0)
    e = pl.program_id(1)
    c0 = cs_ref[e]
    tn = wgb_ref.shape[1]
    wgb_ref[...] = wg_ref[...].astype(BF16)
    wlb_ref[...] = wl_ref[...].astype(BF16)

    def rows_of(c):
        return pl.ds(pl.multiple_of(c * MOE_ROWS, MOE_ROWS), MOE_ROWS)

    def in_copy_abs(c, slot):
        return pltpu.make_async_copy(x_hbm.at[rows_of(c)], xbuf.at[slot], in_sem.at[slot])

    def out_copy_abs(c, slot):
        return pltpu.make_async_copy(obuf.at[slot], act_hbm.at[j, rows_of(c)], out_sem.at[slot])

    def compute(slot):
        x = xbuf[slot]
        for k0 in range(0, tn, MXU_COLS):
            cols = slice(k0, min(k0 + MXU_COLS, tn))
            hg = jnp.dot(x, wgb_ref[:, cols], preferred_element_type=F32) + bg_ref[:, cols]
            hl = jnp.dot(x, wlb_ref[:, cols], preferred_element_type=F32) + bl_ref[:, cols]
            glu = jnp.minimum(hg, SWIGLU_LIMIT)
            lin = jnp.clip(hl, -SWIGLU_LIMIT, SWIGLU_LIMIT)
            act = glu * jax.nn.sigmoid(SWIGLU_ALPHA * glu) * (lin + 1.0)
            obuf[slot, :, cols] = act.astype(obuf.dtype)

    _expert_chunks(nc_ref[e], lambda i, s: in_copy_abs(c0 + i, s), lambda i, s: out_copy_abs(c0 + i, s), compute)

    @pl.when(e == pl.num_programs(1) - 1)
    def _():
        _zero_tail(tail_ref, obuf, out_copy_abs)


def moe_up(xs, w_up, b_up, layer, chunk_start, n_chunks, tail):
    p, d = xs.shape
    n_exp = w_up.shape[1]
    f = w_up.shape[3] // 2
    tn = min(1024, f)
    nj = f // tn
    any_spec = pl.BlockSpec(memory_space=pl.ANY)
    grid_spec = pltpu.PrefetchScalarGridSpec(
        num_scalar_prefetch=3,
        grid=(nj, n_exp),
        in_specs=[any_spec,
                  pl.BlockSpec((None, None, d, tn), lambda j, e, *_: (layer, e, 0, j)),
                  pl.BlockSpec((None, None, d, tn), lambda j, e, *_: (layer, e, 0, nj + j)),
                  pl.BlockSpec((None, None, 1, tn), lambda j, e, *_: (layer, e, 0, j)),
                  pl.BlockSpec((None, None, 1, tn), lambda j, e, *_: (layer, e, 0, nj + j))],
        out_specs=any_spec,
        scratch_shapes=[pltpu.VMEM((d, tn), BF16), pltpu.VMEM((d, tn), BF16),
                        pltpu.VMEM((2, MOE_ROWS, d), BF16), pltpu.VMEM((2, MOE_ROWS, tn), BF16),
                        pltpu.SemaphoreType.DMA((2,)), pltpu.SemaphoreType.DMA((2,))])
    return pl.pallas_call(
        _moe_up_kernel,
        grid_spec=grid_spec,
        out_shape=jax.ShapeDtypeStruct((nj, p, tn), BF16),
        compiler_params=_params(("arbitrary", "arbitrary")),
        name="moe_up",
    )(chunk_start, n_chunks, tail, xs, w_up, w_up, b_up, b_up)


def _moe_down_kernel(cs_ref, nc_ref, tail_ref, a_hbm, w_ref, b_ref, rows_hbm, wb_ref, abuf, obuf, in_sem, out_sem):
    e = pl.program_id(0)
    c0 = cs_ref[e]
    nk, _, tk = a_hbm.shape
    d = wb_ref.shape[1]
    wb_ref[...] = w_ref[...].astype(BF16)

    def rows_of(c):
        return pl.ds(pl.multiple_of(c * MOE_ROWS, MOE_ROWS), MOE_ROWS)

    def in_copy_abs(c, slot):
        return pltpu.make_async_copy(a_hbm.at[:, rows_of(c)], abuf.at[slot], in_sem.at[slot])

    def out_copy_abs(c, slot):
        return pltpu.make_async_copy(obuf.at[slot], rows_hbm.at[rows_of(c)], out_sem.at[slot])

    def compute(slot):
        for k0 in range(0, d, MXU_COLS):
            cols = slice(k0, min(k0 + MXU_COLS, d))
            acc = b_ref[:, cols]
            for kk in range(nk):
                acc = acc + jnp.dot(abuf[slot, kk], wb_ref[kk * tk:(kk + 1) * tk, cols], preferred_element_type=F32)
            obuf[slot, :, cols] = acc

    _expert_chunks(nc_ref[e], lambda i, s: in_copy_abs(c0 + i, s), lambda i, s: out_copy_abs(c0 + i, s), compute)

    @pl.when(e == pl.num_programs(0) - 1)
    def _():
        _zero_tail(tail_ref, obuf, out_copy_abs)


def moe_down(act, w_down, b_down, layer, chunk_start, n_chunks, tail):
    nk, p, tk = act.shape
    n_exp, f, d = w_down.shape[1:]
    any_spec = pl.BlockSpec(memory_space=pl.ANY)
    grid_spec = pltpu.PrefetchScalarGridSpec(
        num_scalar_prefetch=3,
        grid=(n_exp,),
        in_specs=[any_spec,
                  pl.BlockSpec((None, None, f, d), lambda e, *_: (layer, e, 0, 0)),
                  pl.BlockSpec((None, None, 1, d), lambda e, *_: (layer, e, 0, 0))],
        out_specs=any_spec,
        scratch_shapes=[pltpu.VMEM((f, d), BF16),
                        pltpu.VMEM((2, nk, MOE_ROWS, tk), BF16), pltpu.VMEM((2, MOE_ROWS, d), F32),
                        pltpu.SemaphoreType.DMA((2,)), pltpu.SemaphoreType.DMA((2,))])
    return pl.pallas_call(
        _moe_down_kernel,
        grid_spec=grid_spec,
        out_shape=jax.ShapeDtypeStruct((p, d), F32),
        compiler_params=_params(("arbitrary",)),
        name="moe_down",
    )(chunk_start, n_chunks, tail, act, w_down, b_down)


def _combine_kernel(idx_ref, idx_next_ref, gate_ref, rows_ref, o_ref, buf_ref, sem, *, tc):
    i = pl.program_id(0)
    slot = lax.rem(i, 2)

    def copy(sl, k, r, src_row):
        return pltpu.make_async_copy(rows_ref.at[pl.ds(src_row, 1)], buf_ref.at[sl, k, pl.ds(r, 1)], sem.at[sl])

    def start_block(rows_idx_ref, sl):
        for k in range(TOP_K):
            for r in range(tc):
                copy(sl, k, r, rows_idx_ref[0, k * tc + r]).start(priority=r % 2)

    @pl.when(i == 0)
    def _():
        start_block(idx_ref, 0)

    @pl.when(i + 1 < pl.num_programs(0))
    def _():
        start_block(idx_next_ref, 1 - slot)

    for k in range(TOP_K):
        for r in range(tc):
            copy(slot, k, r, 0).wait()
    gate = gate_ref[...]
    acc = buf_ref[slot, 0] * gate[:, 0:1]
    for k in range(1, TOP_K):
        acc = acc + buf_ref[slot, k] * gate[:, k:k + 1]
    o_ref[...] = acc


def moe_combine(rows, dest, gates):
    n = dest.shape[0]
    d = rows.shape[1]
    tc = COMBINE_TOKENS
    assert n % tc == 0
    nblk = n // tc
    idx = dest.reshape(nblk, tc, TOP_K).transpose(0, 2, 1).reshape(nblk, 1, TOP_K * tc)
    return pl.pallas_call(
        functools.partial(_combine_kernel, tc=tc),
        grid=(nblk,),
        in_specs=[pl.BlockSpec((None, 1, TOP_K * tc), lambda i: (i, 0, 0), memory_space=pltpu.SMEM),
                  pl.BlockSpec((None, 1, TOP_K * tc), lambda i: (jnp.minimum(i + 1, nblk - 1), 0, 0),
                               memory_space=pltpu.SMEM),
                  pl.BlockSpec((tc, LANES), lambda i: (i, 0)),
                  pl.BlockSpec(memory_space=pl.ANY)],
        out_specs=pl.BlockSpec((tc, d), lambda i: (i, 0)),
        out_shape=jax.ShapeDtypeStruct((n, d), F32),
        scratch_shapes=[pltpu.VMEM((2, TOP_K, tc, d), F32), pltpu.SemaphoreType.DMA((2,))],
        compiler_params=_params(("arbitrary",)),
        name="moe_combine",
    )(idx, idx, gates, rows)


def moe_ffn(h_all, top_idx, top_gate, w_up, b_up, w_down, b_down, layer):
    n, d = h_all.shape
    n_exp = w_up.shape[1]
    n_pairs = n * TOP_K
    flat_e = top_idx[:, :TOP_K].reshape(-1)
    blk = 256
    assert n_pairs % blk == 0
    onehot = (flat_e[:, None] == jnp.arange(n_exp, dtype=I32)[None, :]).astype(F32).reshape(n_pairs // blk, blk, n_exp)
    within = jnp.einsum('ts,bse->bte', jnp.tril(jnp.ones((blk, blk), F32)), onehot)
    blk_tot = within[:, -1, :]
    blk_end = jnp.cumsum(blk_tot, axis=0)
    csum = within + (blk_end - blk_tot)[:, None, :]
    counts = blk_end[-1].astype(I32)
    rank = jnp.sum((csum - onehot) * onehot, axis=-1).reshape(-1).astype(I32)
    padded = (counts + MOE_ROWS - 1) // MOE_ROWS * MOE_ROWS
    pend = jnp.cumsum(padded)
    pstart = pend - padded
    dest = pstart[flat_e] + rank
    nb = -(-n_pairs // MOE_ROWS) + n_exp
    p = nb * MOE_ROWS
    row_tok = (jnp.arange(p, dtype=I32) % n).at[dest].set(jnp.arange(n_pairs, dtype=I32) // TOP_K,
                                                          unique_indices=True)
    chunk_start = (pstart // MOE_ROWS).astype(I32)
    n_chunks = (padded // MOE_ROWS).astype(I32)
    used = (pend[-1] // MOE_ROWS).astype(I32)
    tail = jnp.stack([used, nb - used])

    xs = gather_rows(h_all, row_tok, BF16)
    act = moe_up(xs, w_up, b_up, layer, chunk_start, n_chunks, tail)
    rows = moe_down(act, w_down, b_down, layer, chunk_start, n_chunks, tail)
    return moe_combine(rows, dest.reshape(n, TOP_K).astype(I32), top_gate)


def _pad_rows_front(a, rows):
    pad = rows - a.shape[1]
    return jnp.pad(a, ((0, 0), (pad, 0), (0, 0)))


def _pad_lanes(a):
    return jnp.pad(a, [(0, 0)] * (a.ndim - 1) + [(0, LANES - a.shape[-1])])


def kernel(x_prompt, x_sample, c_prompt, c_sample, cache_pool, cache_k, cache_v, cache_conv, state_C, state_n, state_m, ab_w_in, ab_w_pool, ab_pool_scale, ab_w_out, ml_w_in, ml_conv_w, ml_conv_b, ml_b_i, ml_b_f, ml_w_out, ada_w, ada_b, ln_g, ln_b, router_w, router_b, moe_w_up, moe_b_up, moe_w_down, moe_b_down):
    bp, tp, d = x_prompt.shape
    bs, ts, _ = x_sample.shape
    depth = ada_w.shape[0]
    alpha = (2 * depth) ** 0.25
    past, sb_heads, sb_dim = cache_k.shape[2:]
    sb_w = sb_heads * sb_dim
    pw = cache_pool.shape[-1]
    ml_heads, dk, dv = state_C.shape[2:]
    qkw = 2 * ml_heads * dk
    vw = ml_heads * dv
    n_exp = router_w.shape[-1]
    mp, ms = bp * tp, bs * ts
    n_tok = mp + ms

    n_c = bp + bs
    c_rows = -(-n_c // 16) * 16
    c_all = jnp.pad(jnp.concatenate([c_prompt, c_sample], axis=0), ((0, c_rows - n_c), (0, 0)))
    mod = adaln(c_all, ada_w, ada_b).reshape(depth, c_rows, 6, d)

    def terms(layer, which):
        rows = slice(0, bp) if which == 0 else slice(bp, bp + bs)
        return [mod[layer, rows, i, :][:, None, :] for i in range(6)]

    router_wp = _pad_lanes(router_w)
    router_bp = _pad_lanes(router_b)[:, None, :]
    b_up4 = moe_b_up[:, :, None, :]
    b_down4 = moe_b_down[:, :, None, :]

    tq = min(512, tp // 2)
    tk_att = min(256, tq)
    tkn = LANES
    tri_n = max(tk_att, min(512, past), tkn)
    tri = (jnp.arange(tri_n)[:, None] > jnp.arange(tri_n)[None, :]).astype(BF16)

    xs_res = [x_prompt.reshape(mp, d), x_sample.reshape(ms, d)]
    trunk_rows = [tp, ts]
    trunk_b = [bp, bs]
    row_off = [0, mp]
    h_mix = [None, None]
    pools, ks, vs, convs, cs, ns, mstates = [[[], []] for _ in range(7)]

    for layer in range(depth):
        j = layer // 2
        t6 = [terms(layer, 0), terms(layer, 1)]
        if layer == 0:
            for w in range(2):
                sh_m, sc_m = t6[w][0], t6[w][1]
                h_mix[w] = modulate(xs_res[w].reshape(trunk_b[w], trunk_rows[w], d), sc_m, sh_m)
        y_mix = [None, None]
        if layer % 2 == 0:
            w_in = ab_w_in[j].astype(BF16)
            w_out = ab_w_out[j].astype(BF16)
            w_pool = ab_w_pool[j].astype(BF16)
            scale = ab_pool_scale[j][None, :]
            for w in range(2):
                bb, tt = trunk_b[w], trunk_rows[w]
                u, q, k, v = [matmul([h_mix[w]], w_in, [0], c0, wd)
                              for c0, wd in ((0, pw), (pw, sb_w), (pw + sb_w, sb_w), (pw + 2 * sb_w, sb_w))]
                u3 = u.reshape(bb, tt, pw)
                if w == 0:
                    hist = jnp.zeros((bb, POOL_HALO, pw), F32)
                    pool_out = pool_mix(u3, hist, w_pool, scale, 0)
                    att = attn_prompt(q.reshape(bb, tt, sb_w), k.reshape(bb, tt, sb_w), v.reshape(bb, tt, sb_w),
                                      tri[:tk_att, :tk_att], sb_heads, tq)
                    pools[w].append(u3[:, tt - (POOL_HALO - 1):])
                else:
                    hist = _pad_rows_front(cache_pool[j], POOL_HALO)
                    pool_out = pool_mix(u3, hist, w_pool, scale, past)
                    z_pool = jnp.concatenate([cache_pool[j], u3], axis=1)
                    pools[w].append(z_pool[:, -(POOL_HALO - 1):])
                    q4 = q.reshape(bb, tt, sb_heads, sb_dim)
                    eye = jnp.eye(sb_heads, dtype=F32)
                    q_bd = jnp.einsum('bihd,hg->bhigd', q4, eye).reshape(bb, sb_heads * tt, sb_w).astype(BF16)
                    k_new = jnp.pad(k.reshape(bb, tt, sb_w), ((0, 0), (0, tkn - tt), (0, 0)))
                    v_new = jnp.pad(v.reshape(bb, tt, sb_w), ((0, 0), (0, tkn - tt), (0, 0)))
                    o_bd = attn_sample(q_bd, k_new, v_new, cache_k[j].reshape(bb, past * sb_heads, sb_dim),
                                       cache_v[j].reshape(bb, past * sb_heads, sb_dim), tri, tt, sb_dim)
                    o5 = o_bd.reshape(bb, sb_heads, tt, sb_heads, sb_dim)
                    att = jnp.einsum('bhihd->bihd', o5).reshape(bb * tt, sb_w).astype(BF16)
                ks[w].append(k.reshape(bb, tt, sb_heads, sb_dim))
                vs[w].append(v.reshape(bb, tt, sb_heads, sb_dim))
                y_mix[w] = matmul([pool_out, att], w_out, [0, pw], 0, d)
        else:
            w_in = ml_w_in[j]
            o1 = qkw
            o2 = o1 + vw
            o3 = o2 + vw
            w_main = w_in[:, :o3].astype(BF16)
            w_gate = _pad_lanes(w_in[:, o3:]).astype(BF16)
            w_out = ml_w_out[j].astype(BF16)
            conv_w = jnp.pad(ml_conv_w[j], ((0, CONV_HALO - CONV_W), (0, 0)))
            conv_b = ml_conv_b[j][None, :]
            gate_b = _pad_lanes(jnp.concatenate([ml_b_i[j], ml_b_f[j]]))[None, :]
            for w in range(2):
                bb, tt = trunk_b[w], trunk_rows[w]
                qk_pre = matmul([h_mix[w]], w_main, [0], 0, qkw).reshape(bb, tt, qkw)
                v_in = matmul([h_mix[w]], w_main, [0], o1, vw).reshape(bb, tt, vw)
                o_pre = matmul([h_mix[w]], w_main, [0], o2, vw).reshape(bb, tt, vw)
                gates = matmul([h_mix[w]], w_gate, [0], 0, LANES, tn=LANES).reshape(bb, tt, LANES)
                if w == 0:
                    hist = jnp.zeros((bb, CONV_HALO, qkw), F32)
                    c0 = jnp.zeros((bb, ml_heads, dk, dv), F32)
                    n0 = jnp.zeros((bb, ml_heads, dk), F32)
                    m0 = jnp.zeros((bb, 1, LANES), F32)
                    chunk = min(256, tt)
                    hid, c1, n1, m1 = mlstm_mix(qk_pre, v_in, o_pre, gates, conv_w, conv_b, gate_b, hist,
                                                c0, n0, m0, chunk, chunk)
                    convs[w].append(qk_pre[:, tt - (CONV_W - 1):])
                else:
                    hist = _pad_rows_front(cache_conv[j], CONV_HALO)
                    m0 = _pad_lanes(state_m[j])[:, None, :]
                    chunk = -(-tt // LANES) * LANES
                    padt = lambda a: jnp.pad(a, ((0, 0), (0, chunk - tt), (0, 0)))
                    hid, c1, n1, m1 = mlstm_mix(padt(qk_pre), padt(v_in), padt(o_pre), padt(gates), conv_w, conv_b,
                                                gate_b, hist, state_C[j], state_n[j], m0, chunk, tt)
                    hid = hid[:, :tt]
                    z_conv = jnp.concatenate([cache_conv[j], qk_pre], axis=1)
                    convs[w].append(z_conv[:, -(CONV_W - 1):])
                cs[w].append(c1)
                ns[w].append(n1)
                mstates[w].append(m1[:, 0, :ml_heads])
                y_mix[w] = matmul([hid.reshape(bb * tt, vw)], w_out, [0], 0, d)

        lg1, lb1 = ln_g[layer, 0][None, :], ln_b[layer, 0][None, :]
        lg2, lb2 = ln_g[layer, 1][None, :], ln_b[layer, 1][None, :]
        router = (router_wp[layer], router_bp[layer], n_exp)
        shared = (jnp.zeros((n_tok, d), F32), jnp.zeros((n_tok, LANES), I32), jnp.zeros((n_tok, LANES), F32))
        x_mid = [None, None]
        for w in range(2):
            g_m, sh_f, sc_f = t6[w][2], t6[w][3], t6[w][4]
            x_mid[w], *shared = deepnorm(xs_res[w], y_mix[w], 0, g_m, lg1, lb1, alpha, trunk_rows[w],
                                         nxt=(sc_f, sh_f), router=router, dest=tuple(shared),
                                         dest_row_off=row_off[w], next_dtype=F32)
        h_all, top_idx, top_gate = shared
        y_ffn = moe_ffn(h_all, top_idx, top_gate, moe_w_up, b_up4, moe_w_down, b_down4, layer)
        for w in range(2):
            g_f = t6[w][5]
            if layer + 1 < depth:
                t_next = terms(layer + 1, w)
                xs_res[w], h_mix[w] = deepnorm(x_mid[w], y_ffn, row_off[w], g_f, lg2, lb2, alpha, trunk_rows[w],
                                               nxt=(t_next[1], t_next[0]))
            else:
                (xs_res[w],) = deepnorm(x_mid[w], y_ffn, row_off[w], g_f, lg2, lb2, alpha, trunk_rows[w])

    outs = []
    for w in range(2):
        outs.append((xs_res[w].reshape(trunk_b[w], trunk_rows[w], d), jnp.stack(pools[w]), jnp.stack(ks[w]),
                     jnp.stack(vs[w]), jnp.stack(convs[w]), jnp.stack(cs[w]), jnp.stack(ns[w]),
                     jnp.stack(mstates[w])))
    (y_p, *rest_p), (y_s, *rest_s) = outs
    return (y_p, y_s, *rest_p, *rest_s)
```

```python
import functools

import jax
import jax.numpy as jnp
from jax import lax
from jax.experimental import pallas as pl
from jax.experimental.pallas import tpu as pltpu

F32 = jnp.float32
BF16 = jnp.bfloat16
I32 = jnp.int32

TOP_K = 4
POOL_WINDOWS = (2, 4, 8, 16)
POOL_HALO = 16
CONV_W = 4
CONV_HALO = 8
SWIGLU_LIMIT = 7.0
SWIGLU_ALPHA = 1.702
LN_EPS = 1e-5
LANES = 128
MXU_COLS = 256
MOE_ROWS = 256
CHUNK_DMA_PRIORITY = 1
GATHER_ROWS = 256
COMBINE_TOKENS = 64
VMEM_LIMIT = 56 * 1024 * 1024
NEG_BIG = -1e30

_HIGHEST = lax.Precision.HIGHEST


def _params(sem, vmem=None):
    return pltpu.CompilerParams(dimension_semantics=sem, vmem_limit_bytes=vmem or VMEM_LIMIT)


def _log_sigmoid(z):
    return jnp.minimum(z, 0.0) - jnp.log(1.0 + jnp.exp(-jnp.abs(z)))


def _adaln_kernel(c_ref, w_ref, b_ref, o_ref):
    c = c_ref[...]
    s = (c * jax.nn.sigmoid(c)).astype(BF16)
    o_ref[...] = jnp.dot(s, w_ref[...].astype(BF16), preferred_element_type=F32) + b_ref[...]


def adaln(c_all, ada_w, ada_b):
    nl, d, n = ada_w.shape
    c = c_all.shape[0]
    tn = min(1024, n)
    return pl.pallas_call(
        _adaln_kernel,
        grid=(nl, n // tn),
        in_specs=[pl.BlockSpec((c, d), lambda l, j: (0, 0)),
                  pl.BlockSpec((None, d, tn), lambda l, j: (l, 0, j)),
                  pl.BlockSpec((None, 1, tn), lambda l, j: (l, 0, j))],
        out_specs=pl.BlockSpec((None, c, tn), lambda l, j: (l, 0, j)),
        out_shape=jax.ShapeDtypeStruct((nl, c, n), F32),
        compiler_params=_params(("arbitrary", "arbitrary")),
        name="adaln",
    )(c_all, ada_w, ada_b.reshape(nl, 1, n))


def _modulate_kernel(x_ref, sc_ref, sh_ref, o_ref):
    o_ref[...] = (x_ref[...] * (1.0 + sc_ref[...]) + sh_ref[...]).astype(o_ref.dtype)


def modulate(x, sc, sh):
    g, r, d = x.shape
    tr = min(512, r)
    out = pl.pallas_call(
        _modulate_kernel,
        grid=(g, r // tr),
        in_specs=[pl.BlockSpec((None, tr, d), lambda b, i: (b, i, 0)),
                  pl.BlockSpec((None, 1, d), lambda b, i: (b, 0, 0)),
                  pl.BlockSpec((None, 1, d), lambda b, i: (b, 0, 0))],
        out_specs=pl.BlockSpec((None, tr, d), lambda b, i: (b, i, 0)),
        out_shape=jax.ShapeDtypeStruct((g, r, d), BF16),
        compiler_params=_params(("arbitrary", "arbitrary")),
        name="modulate",
    )(x, sc, sh)
    return out.reshape(g * r, d)


def _matmul_kernel(*refs, n_a):
    a_refs, w_refs, o_ref = refs[:n_a], refs[n_a:2 * n_a], refs[2 * n_a]
    acc = jnp.dot(a_refs[0][...], w_refs[0][...], preferred_element_type=F32)
    for a, w in zip(a_refs[1:], w_refs[1:]):
        acc = acc + jnp.dot(a[...], w[...], preferred_element_type=F32)
    o_ref[...] = acc.astype(o_ref.dtype)


def matmul(a_list, w, row_offs, col_off, n_cols, out_dtype=F32, tm=512, tn=1024):
    m = a_list[0].shape[0]
    tm = min(tm, m)
    tn = min(tn, n_cols)
    assert m % tm == 0 and n_cols % tn == 0 and col_off % tn == 0
    n_a = len(a_list)
    in_specs = [pl.BlockSpec((tm, a.shape[1]), lambda j, i: (i, 0)) for a in a_list]
    for a, ro in zip(a_list, row_offs):
        ka = a.shape[1]
        assert ro % ka == 0
        in_specs.append(pl.BlockSpec((ka, tn), functools.partial(
            lambda j, i, rb, cb: (rb, cb + j), rb=ro // ka, cb=col_off // tn)))
    return pl.pallas_call(
        functools.partial(_matmul_kernel, n_a=n_a),
        grid=(n_cols // tn, m // tm),
        in_specs=in_specs,
        out_specs=pl.BlockSpec((tm, tn), lambda j, i: (i, j)),
        out_shape=jax.ShapeDtypeStruct((m, n_cols), out_dtype),
        compiler_params=_params(("arbitrary", "arbitrary")),
        name="matmul",
    )(*a_list, *([w] * n_a))


def _pool_kernel(u_ref, hist_ref, w_ref, scale_ref, o_ref, z_ref, *, tt, pg, pos0):
    ti = pl.program_id(1)

    @pl.when(ti == 0)
    def _():
        z_ref[0:POOL_HALO, :] = hist_ref[...]

    @pl.when(ti > 0)
    def _():
        z_ref[0:POOL_HALO, :] = z_ref[tt:tt + POOL_HALO, :]

    z_ref[POOL_HALO:POOL_HALO + tt, :] = u_ref[...]
    pos = pos0 + ti * tt + lax.broadcasted_iota(I32, (tt, 1), 0)
    for g, w in enumerate(POOL_WINDOWS):
        cols = slice(g * pg, (g + 1) * pg)
        u_g = z_ref[POOL_HALO:POOL_HALO + tt, cols]
        acc = u_g
        for j in range(1, w):
            acc = acc + z_ref[POOL_HALO - j:POOL_HALO - j + tt, cols]
        cnt = jnp.minimum(w, pos + 1).astype(F32)
        diff = (acc / cnt - u_g).astype(BF16)
        y = jnp.dot(diff, w_ref[g], preferred_element_type=F32) * scale_ref[:, cols]
        o_ref[:, cols] = y.astype(o_ref.dtype)


def pool_mix(u, hist, w_pool, scale, pos0):
    b, t, pw = u.shape
    ng, pg, _ = w_pool.shape
    tt = min(512, t)
    assert tt >= POOL_HALO and t % tt == 0 and ng == len(POOL_WINDOWS)
    out = pl.pallas_call(
        functools.partial(_pool_kernel, tt=tt, pg=pg, pos0=pos0),
        grid=(b, t // tt),
        in_specs=[pl.BlockSpec((None, tt, pw), lambda i, j: (i, j, 0)),
                  pl.BlockSpec((None, POOL_HALO, pw), lambda i, j: (i, 0, 0)),
                  pl.BlockSpec((ng, pg, pg), lambda i, j: (0, 0, 0)),
                  pl.BlockSpec((1, pw), lambda i, j: (0, 0))],
        out_specs=pl.BlockSpec((None, tt, pw), lambda i, j: (i, j, 0)),
        out_shape=jax.ShapeDtypeStruct((b, t, pw), BF16),
        scratch_shapes=[pltpu.VMEM((tt + POOL_HALO, pw), F32)],
        compiler_params=_params(("arbitrary", "arbitrary")),
        name="pool_mix",
    )(u, hist, w_pool, scale)
    return out.reshape(b * t, pw)


def _sb_block(z, lsz_mask, tri, v_bf, r_prev):
    lsz = _log_sigmoid(z)
    lk = lsz - z
    if lsz_mask is not None:
        lk = jnp.where(lsz_mask, lk, 0.0)
    hi = lk.astype(BF16)
    lo = (lk - hi.astype(F32)).astype(BF16)
    after = (jnp.dot(hi, tri, preferred_element_type=F32)
             + jnp.dot(lo, tri, preferred_element_type=F32))
    a = jnp.exp(lsz + after + r_prev)
    if lsz_mask is not None:
        a = jnp.where(lsz_mask, a, 0.0)
    pv = jnp.dot(a.astype(BF16), v_bf, preferred_element_type=F32)
    return pv, jnp.sum(lk, axis=1, keepdims=True)


def _attn_prompt_kernel(q_ref, k_ref, v_ref, tri_ref, o_ref, kb_ref, vb_ref, r_ref, acc_ref, *, scale, tq, tk):
    qi = pl.program_id(2)

    @pl.when(qi == 0)
    def _():
        kb_ref[...] = k_ref[...].astype(BF16)
        vb_ref[...] = v_ref[...].astype(BF16)

    q = q_ref[...].astype(BF16)
    r_ref[...] = jnp.zeros_like(r_ref)
    acc_ref[...] = jnp.zeros_like(acc_ref)

    def key_block(k0, masked):
        for sub in reversed(range(tq // tk)):
            ks = pl.multiple_of(k0 + sub * tk, tk)
            z = lax.dot_general(q, kb_ref[pl.ds(ks, tk), :], (((1,), (1,)), ((), ())),
                                preferred_element_type=F32) * scale
            mask = None
            if masked:
                row = lax.broadcasted_iota(I32, (tq, tk), 0)
                col = lax.broadcasted_iota(I32, (tq, tk), 1) + sub * tk
                mask = col < row
            pv, lk_sum = _sb_block(z, mask, tri_ref[...], vb_ref[pl.ds(ks, tk), :], r_ref[...])
            acc_ref[...] += pv
            r_ref[...] += lk_sum

    key_block(qi * tq, True)

    def body(i, carry):
        key_block((qi - 1 - i) * tq, False)
        return carry

    lax.fori_loop(0, qi, body, 0)
    o_ref[...] = acc_ref[...].astype(o_ref.dtype)


def attn_prompt(q, k, v, tri, heads, tq):
    b, t, w = q.shape
    d = w // heads
    tk = tri.shape[0]
    assert t % tq == 0 and tq % tk == 0
    kv_spec = pl.BlockSpec((None, t, d), lambda bi, h, qi: (bi, 0, h))
    out = pl.pallas_call(
        functools.partial(_attn_prompt_kernel, scale=d ** -0.5, tq=tq, tk=tk),
        grid=(b, heads, t // tq),
        in_specs=[pl.BlockSpec((None, tq, d), lambda bi, h, qi: (bi, qi, h)),
                  kv_spec, kv_spec,
                  pl.BlockSpec((tk, tk), lambda bi, h, qi: (0, 0))],
        out_specs=pl.BlockSpec((None, tq, d), lambda bi, h, qi: (bi, qi, h)),
        out_shape=jax.ShapeDtypeStruct((b, t, w), BF16),
        scratch_shapes=[pltpu.VMEM((t, d), BF16), pltpu.VMEM((t, d), BF16),
                        pltpu.VMEM((tq, 1), F32), pltpu.VMEM((tq, d), F32)],
        compiler_params=_params(("arbitrary",) * 3),
        name="attn_prompt",
    )(q, k, v, tri)
    return out.reshape(b * t, w)


def _attn_sample_kernel(q_ref, kn_ref, vn_ref, kp_ref, vp_ref, tri_ref, o_ref, r_ref, *, scale, ts, tkn, tkp, heads):
    j = pl.program_id(1)
    rows = q_ref.shape[0]

    def block(k, v, tk, masked):
        z = lax.dot_general(q_ref[...], k, (((1,), (1,)), ((), ())), preferred_element_type=F32) * scale
        mask = None
        if masked:
            row = lax.rem(lax.broadcasted_iota(I32, (rows, tk), 0), ts)
            col = lax.broadcasted_iota(I32, (rows, tk), 1)
            mask = col < row
        pv, lk_sum = _sb_block(z, mask, tri_ref[0:tk, 0:tk], v, r_ref[...])
        o_ref[...] += pv
        r_ref[...] += lk_sum

    def heads_on_lanes(ref):
        return jnp.concatenate([ref[pl.ds(h, tkp, stride=heads), :].astype(BF16) for h in range(heads)], axis=1)

    @pl.when(j == 0)
    def _():
        r_ref[...] = jnp.zeros_like(r_ref)
        o_ref[...] = jnp.zeros_like(o_ref)
        block(kn_ref[...].astype(BF16), vn_ref[...].astype(BF16), tkn, True)

    @pl.when(j > 0)
    def _():
        block(heads_on_lanes(kp_ref), heads_on_lanes(vp_ref), tkp, False)


def attn_sample(q_bd, k_new, v_new, k_past, v_past, tri, ts, d):
    b, rows, w = q_bd.shape
    heads = w // d
    tkn = k_new.shape[1]
    past = k_past.shape[1] // heads
    tkp = min(512, past)
    assert past % tkp == 0 and tri.shape[0] >= max(tkn, tkp)
    n_past = past // tkp
    past_spec = pl.BlockSpec((None, tkp * heads, d), lambda bi, j: (bi, n_past - jnp.maximum(j, 1), 0))
    new_spec = pl.BlockSpec((None, tkn, w), lambda bi, j: (bi, 0, 0))
    return pl.pallas_call(
        functools.partial(_attn_sample_kernel, scale=d ** -0.5, ts=ts, tkn=tkn, tkp=tkp, heads=heads),
        grid=(b, 1 + n_past),
        in_specs=[pl.BlockSpec((None, rows, w), lambda bi, j: (bi, 0, 0)),
                  new_spec, new_spec, past_spec, past_spec,
                  pl.BlockSpec(tri.shape, lambda bi, j: (0, 0))],
        out_specs=pl.BlockSpec((None, rows, w), lambda bi, j: (bi, 0, 0)),
        out_shape=jax.ShapeDtypeStruct((b, rows, w), F32),
        scratch_shapes=[pltpu.VMEM((rows, 1), F32)],
        compiler_params=_params(("arbitrary", "arbitrary")),
        name="attn_sample",
    )(q_bd, k_new, v_new, k_past, v_past, tri)


def _mlstm_kernel(qk_ref, v_ref, op_ref, g_ref, cw_ref, cb_ref, gb_ref, hist_ref, c0_ref, n0_ref, m0_ref,
                  tril_ref, triu_ref,
                  hid_ref, c1_ref, n1_ref, m1_ref,
                  z_ref, c_ref, n_ref, m_ref, *, L, H, Dk, Dv, n_valid, kscale):
    c = pl.program_id(1)
    qw = H * Dk

    @pl.when(c == 0)
    def _():
        z_ref[0:CONV_HALO, :] = hist_ref[...]
        c_ref[...] = c0_ref[...]
        n_ref[...] = n0_ref[...]
        m_ref[...] = m0_ref[...]

    @pl.when(c > 0)
    def _():
        z_ref[0:CONV_HALO, :] = z_ref[L:L + CONV_HALO, :]

    z_ref[CONV_HALO:CONV_HALO + L, :] = qk_ref[...]
    y = cb_ref[...]
    for j in range(CONV_W):
        r0 = CONV_HALO - (CONV_W - 1) + j
        y = y + z_ref[r0:r0 + L, :] * cw_ref[j:j + 1, :]
    qk = y * jax.nn.sigmoid(y)

    g = g_ref[...] + gb_ref[...]
    lane = lax.broadcasted_iota(I32, (L, LANES), 1)
    gl = jnp.where(jnp.logical_and(lane >= H, lane < 2 * H), _log_sigmoid(g), g)
    if n_valid < L:
        row = lax.broadcasted_iota(I32, (L, LANES), 0)
        gl = jnp.where(row < n_valid, gl, jnp.where(lane < H, NEG_BIG, 0.0))
    glt = gl.T
    b_col = jnp.dot(tril_ref[...], gl, preferred_element_type=F32, precision=_HIGHEST)
    b_row = jnp.dot(glt, triu_ref[...], preferred_element_type=F32, precision=_HIGHEST)
    trow = lax.broadcasted_iota(I32, (L, L), 0)
    tcol = lax.broadcasted_iota(I32, (L, L), 1)
    causal = tcol <= trow

    for h in range(H):
        m_prev = m_ref[:, h:h + 1]
        bc = b_col[:, H + h:H + h + 1]
        br = b_row[H + h:H + h + 1, :]
        igc = gl[:, h:h + 1]
        igr = glt[h:h + 1, :]
        dm = jnp.where(causal, bc - br + igr, -jnp.inf)
        inter = bc + m_prev
        m_t = jnp.maximum(inter, jnp.max(dm, axis=1, keepdims=True))
        qh = qk[:, h * Dk:(h + 1) * Dk]
        kh = qk[:, qw + h * Dk:qw + (h + 1) * Dk] * kscale
        qb = qh.astype(BF16)
        vb = v_ref[:, h * Dv:(h + 1) * Dv].astype(BF16)
        s = jnp.exp(dm - m_t) * lax.dot_general(qb, kh.astype(BF16), (((1,), (1,)), ((), ())),
                                                 preferred_element_type=F32)
        decay = jnp.exp(inter - m_t)
        ch = c_ref[h]
        nh = n_ref[h:h + 1, :]
        num = (jnp.dot(s.astype(BF16), vb, preferred_element_type=F32)
               + decay * jnp.dot(qb, ch.astype(BF16), preferred_element_type=F32))
        den = jnp.sum(s, axis=1, keepdims=True) + decay * jnp.sum(qh * nh, axis=1, keepdims=True)
        hval = num / jnp.maximum(jnp.abs(den), jnp.exp(-m_t))
        og = jax.nn.sigmoid(op_ref[:, h * Dv:(h + 1) * Dv])
        hid_ref[:, h * Dv:(h + 1) * Dv] = (og * hval).astype(hid_ref.dtype)

        b_last = bc[L - 1:L, :]
        gc = b_last - bc + igc
        m_new = jnp.maximum(b_last + m_prev, jnp.max(gc, axis=0, keepdims=True))
        kw = kh * jnp.exp(gc - m_new)
        keep = jnp.exp(b_last + m_prev - m_new)
        c_ref[h] = keep * ch + jnp.dot(kw.T.astype(BF16), vb, preferred_element_type=F32)
        n_ref[h:h + 1, :] = keep * nh + jnp.sum(kw, axis=0, keepdims=True)
        m_ref[:, h:h + 1] = m_new

    @pl.when(c == pl.num_programs(1) - 1)
    def _():
        c1_ref[...] = c_ref[...]
        n1_ref[...] = n_ref[...]
        m1_ref[...] = m_ref[...]


def mlstm_mix(qk_pre, v, o_pre, gates, conv_w, conv_b, gate_b, hist, c0, n0, m0, L, n_valid):
    b, t, qk2 = qk_pre.shape
    _, hh, dk, dv = c0.shape
    vw = hh * dv
    assert t % L == 0
    tril = jnp.tril(jnp.ones((L, L), F32))
    row3 = lambda i, j: (i, j, 0)
    fix3 = lambda i, j: (i, 0, 0)
    fix2 = lambda i, j: (0, 0)
    return pl.pallas_call(
        functools.partial(_mlstm_kernel, L=L, H=hh, Dk=dk, Dv=dv, n_valid=n_valid, kscale=dk ** -0.5),
        grid=(b, t // L),
        in_specs=[pl.BlockSpec((None, L, qk2), row3),
                  pl.BlockSpec((None, L, vw), row3),
                  pl.BlockSpec((None, L, vw), row3),
                  pl.BlockSpec((None, L, LANES), row3),
                  pl.BlockSpec(conv_w.shape, fix2),
                  pl.BlockSpec(conv_b.shape, fix2),
                  pl.BlockSpec(gate_b.shape, fix2),
                  pl.BlockSpec((None, CONV_HALO, qk2), fix3),
                  pl.BlockSpec((None, hh, dk, dv), lambda i, j: (i, 0, 0, 0)),
                  pl.BlockSpec((None, hh, dk), fix3),
                  pl.BlockSpec((None, 1, LANES), fix3),
                  pl.BlockSpec((L, L), fix2),
                  pl.BlockSpec((L, L), fix2)],
        out_specs=[pl.BlockSpec((None, L, vw), row3),
                   pl.BlockSpec((None, hh, dk, dv), lambda i, j: (i, 0, 0, 0)),
                   pl.BlockSpec((None, hh, dk), fix3),
                   pl.BlockSpec((None, 1, LANES), fix3)],
        out_shape=[jax.ShapeDtypeStruct((b, t, vw), BF16),
                   jax.ShapeDtypeStruct(c0.shape, F32),
                   jax.ShapeDtypeStruct(n0.shape, F32),
                   jax.ShapeDtypeStruct(m0.shape, F32)],
        scratch_shapes=[pltpu.VMEM((L + CONV_HALO, qk2), F32),
                        pltpu.VMEM((hh, dk, dv), F32),
                        pltpu.VMEM((hh, dk), F32),
                        pltpu.VMEM((1, LANES), F32)],
        compiler_params=_params(("arbitrary", "arbitrary")),
        name="mlstm_mix",
    )(qk_pre, v, o_pre, gates, conv_w, conv_b, gate_b, hist, c0, n0, m0, tril, tril.T)


def _ln_kernel(*refs, alpha, with_next, with_router, n_experts, next_dtype):
    x_ref, y_ref, gate_ref, lg_ref, lb_ref = refs[:5]
    pos = 5
    if with_next:
        sc_ref, sh_ref = refs[pos:pos + 2]
        pos += 2
    if with_router:
        rw_ref, rb_ref = refs[pos:pos + 2]
        pos += 2
    n_out = 1 + int(with_next) + 2 * int(with_router)
    outs = refs[len(refs) - n_out:]
    xo_ref = outs[0]

    v = alpha * x_ref[...] + (1.0 + gate_ref[...]) * y_ref[...]
    mu = jnp.mean(v, axis=-1, keepdims=True)
    vc = v - mu
    var = jnp.mean(vc * vc, axis=-1, keepdims=True)
    xn = vc * lax.rsqrt(var + LN_EPS) * lg_ref[...] + lb_ref[...]
    xo_ref[...] = xn
    if not with_next:
        return
    h = xn * (1.0 + sc_ref[...]) + sh_ref[...]
    outs[1][...] = h.astype(next_dtype)
    if not with_router:
        return
    tm = h.shape[0]
    logits = jnp.dot(h.astype(BF16), rw_ref[...].astype(BF16), preferred_element_type=F32) + rb_ref[...]
    lane = lax.broadcasted_iota(I32, (tm, LANES), 1).astype(F32)
    logits = jnp.where(lane < n_experts, logits, -jnp.inf)
    idx_out = jnp.zeros((tm, LANES), F32)
    gate_out = jnp.zeros((tm, LANES), F32)
    top0 = None
    for k in range(TOP_K):
        mk = jnp.max(logits, axis=1, keepdims=True)
        ik = jnp.min(jnp.where(logits == mk, lane, float(LANES)), axis=1, keepdims=True)
        logits = jnp.where(lane == ik, -jnp.inf, logits)
        if top0 is None:
            top0 = mk
        idx_out = jnp.where(lane == k, ik, idx_out)
        gate_out = jnp.where(lane == k, jnp.exp(mk - top0), gate_out)
    gate_out = gate_out / jnp.sum(gate_out, axis=1, keepdims=True)
    outs[2][...] = idx_out.astype(I32)
    outs[3][...] = gate_out


def deepnorm(x, y, y_row_off, gate, ln_g, ln_b, alpha, rows_per_group, nxt=None, router=None,
             dest=None, dest_row_off=0, next_dtype=BF16):
    m, d = x.shape
    tm = min(256, rows_per_group)
    assert rows_per_group % tm == 0 and y_row_off % tm == 0 and dest_row_off % tm == 0
    bpg = rows_per_group // tm
    yo = y_row_off // tm
    do = dest_row_off // tm
    row = lambda i: (i, 0)
    grp = lambda i: (i // bpg, 0, 0)
    fix = lambda i: (0, 0)
    args = [x, y, gate, ln_g, ln_b]
    in_specs = [pl.BlockSpec((tm, d), row), pl.BlockSpec((tm, d), lambda i: (i + yo, 0)),
                pl.BlockSpec((None, 1, d), grp), pl.BlockSpec((1, d), fix), pl.BlockSpec((1, d), fix)]
    out_shape = [jax.ShapeDtypeStruct((m, d), F32)]
    out_specs = [pl.BlockSpec((tm, d), row)]
    n_experts = 0
    if nxt is not None:
        args += list(nxt)
        in_specs += [pl.BlockSpec((None, 1, d), grp)] * 2
    if router is not None:
        rw, rb, n_experts = router
        args += [rw, rb]
        in_specs += [pl.BlockSpec(rw.shape, fix), pl.BlockSpec(rb.shape, fix)]
    aliases = {}
    dst = lambda i: (i + do, 0)
    if nxt is not None:
        widths = [(d, next_dtype)] + ([(LANES, I32), (LANES, F32)] if router is not None else [])
        for k, (wd, dt) in enumerate(widths):
            if dest is not None:
                aliases[len(args)] = 1 + k
                args.append(dest[k])
                in_specs.append(pl.BlockSpec(memory_space=pl.ANY))
                out_shape.append(jax.ShapeDtypeStruct(dest[k].shape, dt))
            else:
                out_shape.append(jax.ShapeDtypeStruct((m, wd), dt))
            out_specs.append(pl.BlockSpec((tm, wd), dst))
    return pl.pallas_call(
        functools.partial(_ln_kernel, alpha=alpha, with_next=nxt is not None, with_router=router is not None,
                          n_experts=n_experts, next_dtype=next_dtype),
        grid=(m // tm,),
        in_specs=in_specs,
        out_specs=out_specs,
        out_shape=out_shape,
        input_output_aliases=aliases,
        compiler_params=_params(("arbitrary",)),
        name="deepnorm",
    )(*args)


def _gather_kernel(idx_ref, idx_next_ref, src_ref, o_ref, buf_ref, sem, *, rb):
    i = pl.program_id(0)
    slot = lax.rem(i, 2)

    def copy(sl, r, src_row):
        return pltpu.make_async_copy(src_ref.at[pl.ds(src_row, 1)], buf_ref.at[sl, pl.ds(r, 1)], sem.at[sl])

    def start_block(rows_ref, sl):
        for r in range(rb):
            copy(sl, r, rows_ref[0, r]).start(priority=r % 2)

    @pl.when(i == 0)
    def _():
        start_block(idx_ref, 0)

    @pl.when(i + 1 < pl.num_programs(0))
    def _():
        start_block(idx_next_ref, 1 - slot)

    for r in range(rb):
        copy(slot, r, 0).wait()
    o_ref[...] = buf_ref[slot].astype(o_ref.dtype)


def gather_rows(src, idx, out_dtype):
    p = idx.shape[0]
    d = src.shape[1]
    rb = GATHER_ROWS
    assert p % rb == 0
    nblk = p // rb
    idx3 = idx.reshape(nblk, 1, rb)
    return pl.pallas_call(
        functools.partial(_gather_kernel, rb=rb),
        grid=(nblk,),
        in_specs=[pl.BlockSpec((None, 1, rb), lambda i: (i, 0, 0), memory_space=pltpu.SMEM),
                  pl.BlockSpec((None, 1, rb), lambda i: (jnp.minimum(i + 1, nblk - 1), 0, 0),
                               memory_space=pltpu.SMEM),
                  pl.BlockSpec(memory_space=pl.ANY)],
        out_specs=pl.BlockSpec((rb, d), lambda i: (i, 0)),
        out_shape=jax.ShapeDtypeStruct((p, d), out_dtype),
        scratch_shapes=[pltpu.VMEM((2, rb, d), src.dtype), pltpu.SemaphoreType.DMA((2,))],
        compiler_params=_params(("arbitrary",)),
        name="moe_gather",
    )(idx3, idx3, src)


def _expert_chunks(n, in_copy, out_copy, compute, first_step, prefetch_next):
    @pl.when(jnp.logical_and(first_step, n > 0))
    def _():
        in_copy(0, 0).start(priority=CHUNK_DMA_PRIORITY)

    def body(i, carry):
        slot = lax.rem(i, 2)
        in_copy(i, slot).wait()

        @pl.when(i + 1 < n)
        def _():
            in_copy(i + 1, 1 - slot).start(priority=CHUNK_DMA_PRIORITY)

        @pl.when(i >= 2)
        def _():
            out_copy(i - 2, slot).wait()

        compute(slot)
        out_copy(i, slot).start(priority=CHUNK_DMA_PRIORITY)
        return carry

    lax.fori_loop(0, n, body, 0)
    prefetch_next()

    @pl.when(n >= 2)
    def _():
        out_copy(n - 2, lax.rem(n, 2)).wait()

    @pl.when(n >= 1)
    def _():
        out_copy(n - 1, lax.rem(n + 1, 2)).wait()


def _zero_tail(tail_ref, buf, out_copy_abs):
    buf[0] = jnp.zeros(buf.shape[1:], buf.dtype)
    t0, nt = tail_ref[0], tail_ref[1]

    def start(i, carry):
        out_copy_abs(t0 + i, 0).start()
        return carry

    def wait(i, carry):
        out_copy_abs(t0 + i, 0).wait()
        return carry

    lax.fori_loop(0, nt, start, 0)
    lax.fori_loop(0, nt, wait, 0)


def _moe_up_kernel(cs_ref, nc_ref, tail_ref, x_hbm, wg_ref, wl_ref, bg_ref, bl_ref, act_hbm,
                   wgb_ref, wlb_ref, xbuf, obuf, in_sem, out_sem):
    j = pl.program_id(0)
    e = pl.program_id(1)
    c0 = cs_ref[e]
    tn = wgb_ref.shape[1]
    wgb_ref[...] = wg_ref[...].astype(BF16)
    wlb_ref[...] = wl_ref[...].astype(BF16)

    def rows_of(c):
        return pl.ds(pl.multiple_of(c * MOE_ROWS, MOE_ROWS), MOE_ROWS)

    def in_copy_abs(c, slot):
        return pltpu.make_async_copy(x_hbm.at[rows_of(c)], xbuf.at[slot], in_sem.at[slot])

    def out_copy_abs(c, slot):
        return pltpu.make_async_copy(obuf.at[slot], act_hbm.at[j, rows_of(c)], out_sem.at[slot])

    def compute(slot):
        x = xbuf[slot]
        for k0 in range(0, tn, MXU_COLS):
            cols = slice(k0, min(k0 + MXU_COLS, tn))
            hg = jnp.dot(x, wgb_ref[:, cols], preferred_element_type=F32) + bg_ref[:, cols]
            hl = jnp.dot(x, wlb_ref[:, cols], preferred_element_type=F32) + bl_ref[:, cols]
            glu = jnp.minimum(hg, SWIGLU_LIMIT)
            lin = jnp.clip(hl, -SWIGLU_LIMIT, SWIGLU_LIMIT)
            act = glu * jax.nn.sigmoid(SWIGLU_ALPHA * glu) * (lin + 1.0)
            obuf[slot, :, cols] = act.astype(obuf.dtype)

    n_exp = pl.num_programs(1)
    step = j * n_exp + e
    e_next = jnp.where(e + 1 < n_exp, e + 1, 0)

    def prefetch_next():
        @pl.when(jnp.logical_and(step + 1 < pl.num_programs(0) * n_exp, nc_ref[e_next] > 0))
        def _():
            in_copy_abs(cs_ref[e_next], 0).start(priority=CHUNK_DMA_PRIORITY)

    _expert_chunks(nc_ref[e], lambda i, s: in_copy_abs(c0 + i, s), lambda i, s: out_copy_abs(c0 + i, s), compute,
                   step == 0, prefetch_next)

    @pl.when(e == n_exp - 1)
    def _():
        _zero_tail(tail_ref, obuf, out_copy_abs)


def moe_up(xs, w_up, b_up, layer, chunk_start, n_chunks, tail):
    p, d = xs.shape
    n_exp = w_up.shape[1]
    f = w_up.shape[3] // 2
    tn = min(1024, f)
    nj = f // tn
    any_spec = pl.BlockSpec(memory_space=pl.ANY)
    grid_spec = pltpu.PrefetchScalarGridSpec(
        num_scalar_prefetch=3,
        grid=(nj, n_exp),
        in_specs=[any_spec,
                  pl.BlockSpec((None, None, d, tn), lambda j, e, *_: (layer, e, 0, j)),
                  pl.BlockSpec((None, None, d, tn), lambda j, e, *_: (layer, e, 0, nj + j)),
                  pl.BlockSpec((None, None, 1, tn), lambda j, e, *_: (layer, e, 0, j)),
                  pl.BlockSpec((None, None, 1, tn), lambda j, e, *_: (layer, e, 0, nj + j))],
        out_specs=any_spec,
        scratch_shapes=[pltpu.VMEM((d, tn), BF16), pltpu.VMEM((d, tn), BF16),
                        pltpu.VMEM((2, MOE_ROWS, d), BF16), pltpu.VMEM((2, MOE_ROWS, tn), BF16),
                        pltpu.SemaphoreType.DMA((2,)), pltpu.SemaphoreType.DMA((2,))])
    return pl.pallas_call(
        _moe_up_kernel,
        grid_spec=grid_spec,
        out_shape=jax.ShapeDtypeStruct((nj, p, tn), BF16),
        compiler_params=_params(("arbitrary", "arbitrary")),
        name="moe_up",
    )(chunk_start, n_chunks, tail, xs, w_up, w_up, b_up, b_up)


def _moe_down_kernel(cs_ref, nc_ref, tail_ref, a_hbm, w_ref, b_ref, rows_hbm, wb_ref, abuf, obuf, in_sem, out_sem):
    e = pl.program_id(0)
    c0 = cs_ref[e]
    nk, _, tk = a_hbm.shape
    d = wb_ref.shape[1]
    wb_ref[...] = w_ref[...].astype(BF16)

    def rows_of(c):
        return pl.ds(pl.multiple_of(c * MOE_ROWS, MOE_ROWS), MOE_ROWS)

    def in_copy_abs(c, slot):
        return pltpu.make_async_copy(a_hbm.at[:, rows_of(c)], abuf.at[slot], in_sem.at[slot])

    def out_copy_abs(c, slot):
        return pltpu.make_async_copy(obuf.at[slot], rows_hbm.at[rows_of(c)], out_sem.at[slot])

    def compute(slot):
        for k0 in range(0, d, MXU_COLS):
            cols = slice(k0, min(k0 + MXU_COLS, d))
            acc = b_ref[:, cols]
            for kk in range(nk):
                acc = acc + jnp.dot(abuf[slot, kk], wb_ref[kk * tk:(kk + 1) * tk, cols], preferred_element_type=F32)
            obuf[slot, :, cols] = acc

    n_exp = pl.num_programs(0)
    e_next = jnp.minimum(e + 1, n_exp - 1)

    def prefetch_next():
        @pl.when(jnp.logical_and(e + 1 < n_exp, nc_ref[e_next] > 0))
        def _():
            in_copy_abs(cs_ref[e_next], 0).start(priority=CHUNK_DMA_PRIORITY)

    _expert_chunks(nc_ref[e], lambda i, s: in_copy_abs(c0 + i, s), lambda i, s: out_copy_abs(c0 + i, s), compute,
                   e == 0, prefetch_next)

    @pl.when(e == n_exp - 1)
    def _():
        _zero_tail(tail_ref, obuf, out_copy_abs)


def moe_down(act, w_down, b_down, layer, chunk_start, n_chunks, tail):
    nk, p, tk = act.shape
    n_exp, f, d = w_down.shape[1:]
    any_spec = pl.BlockSpec(memory_space=pl.ANY)
    grid_spec = pltpu.PrefetchScalarGridSpec(
        num_scalar_prefetch=3,
        grid=(n_exp,),
        in_specs=[any_spec,
                  pl.BlockSpec((None, None, f, d), lambda e, *_: (layer, e, 0, 0)),
                  pl.BlockSpec((None, None, 1, d), lambda e, *_: (layer, e, 0, 0))],
        out_specs=any_spec,
        scratch_shapes=[pltpu.VMEM((f, d), BF16),
                        pltpu.VMEM((2, nk, MOE_ROWS, tk), BF16), pltpu.VMEM((2, MOE_ROWS, d), F32),
                        pltpu.SemaphoreType.DMA((2,)), pltpu.SemaphoreType.DMA((2,))])
    return pl.pallas_call(
        _moe_down_kernel,
        grid_spec=grid_spec,
        out_shape=jax.ShapeDtypeStruct((p, d), F32),
        compiler_params=_params(("arbitrary",)),
        name="moe_down",
    )(chunk_start, n_chunks, tail, act, w_down, b_down)


def _combine_kernel(idx_ref, idx_next_ref, gate_ref, rows_ref, o_ref, buf_ref, sem, *, tc):
    i = pl.program_id(0)
    slot = lax.rem(i, 2)

    def copy(sl, k, r, src_row):
        return pltpu.make_async_copy(rows_ref.at[pl.ds(src_row, 1)], buf_ref.at[sl, k, pl.ds(r, 1)], sem.at[sl])

    def start_block(rows_idx_ref, sl):
        for k in range(TOP_K):
            for r in range(tc):
                copy(sl, k, r, rows_idx_ref[0, k * tc + r]).start(priority=r % 2)

    @pl.when(i == 0)
    def _():
        start_block(idx_ref, 0)

    @pl.when(i + 1 < pl.num_programs(0))
    def _():
        start_block(idx_next_ref, 1 - slot)

    for k in range(TOP_K):
        for r in range(tc):
            copy(slot, k, r, 0).wait()
    gate = gate_ref[...]
    acc = buf_ref[slot, 0] * gate[:, 0:1]
    for k in range(1, TOP_K):
        acc = acc + buf_ref[slot, k] * gate[:, k:k + 1]
    o_ref[...] = acc


def moe_combine(rows, dest, gates):
    n = dest.shape[0]
    d = rows.shape[1]
    tc = COMBINE_TOKENS
    assert n % tc == 0
    nblk = n // tc
    idx = dest.reshape(nblk, tc, TOP_K).transpose(0, 2, 1).reshape(nblk, 1, TOP_K * tc)
    return pl.pallas_call(
        functools.partial(_combine_kernel, tc=tc),
        grid=(nblk,),
        in_specs=[pl.BlockSpec((None, 1, TOP_K * tc), lambda i: (i, 0, 0), memory_space=pltpu.SMEM),
                  pl.BlockSpec((None, 1, TOP_K * tc), lambda i: (jnp.minimum(i + 1, nblk - 1), 0, 0),
                               memory_space=pltpu.SMEM),
                  pl.BlockSpec((tc, LANES), lambda i: (i, 0)),
                  pl.BlockSpec(memory_space=pl.ANY)],
        out_specs=pl.BlockSpec((tc, d), lambda i: (i, 0)),
        out_shape=jax.ShapeDtypeStruct((n, d), F32),
        scratch_shapes=[pltpu.VMEM((2, TOP_K, tc, d), F32), pltpu.SemaphoreType.DMA((2,))],
        compiler_params=_params(("arbitrary",)),
        name="moe_combine",
    )(idx, idx, gates, rows)


def moe_ffn(h_all, top_idx, top_gate, w_up, b_up, w_down, b_down, layer):
    n, d = h_all.shape
    n_exp = w_up.shape[1]
    n_pairs = n * TOP_K
    flat_e = top_idx[:, :TOP_K].reshape(-1)
    blk = 256
    assert n_pairs % blk == 0
    onehot = (flat_e[:, None] == jnp.arange(n_exp, dtype=I32)[None, :]).astype(F32).reshape(n_pairs // blk, blk, n_exp)
    within = jnp.einsum('ts,bse->bte', jnp.tril(jnp.ones((blk, blk), F32)), onehot)
    blk_tot = within[:, -1, :]
    blk_end = jnp.cumsum(blk_tot, axis=0)
    csum = within + (blk_end - blk_tot)[:, None, :]
    counts = blk_end[-1].astype(I32)
    rank = jnp.sum((csum - onehot) * onehot, axis=-1).reshape(-1).astype(I32)
    padded = (counts + MOE_ROWS - 1) // MOE_ROWS * MOE_ROWS
    pend = jnp.cumsum(padded)
    pstart = pend - padded
    dest = pstart[flat_e] + rank
    nb = -(-n_pairs // MOE_ROWS) + n_exp
    p = nb * MOE_ROWS
    row_tok = (jnp.arange(p, dtype=I32) % n).at[dest].set(jnp.arange(n_pairs, dtype=I32) // TOP_K,
                                                          unique_indices=True)
    chunk_start = (pstart // MOE_ROWS).astype(I32)
    n_chunks = (padded // MOE_ROWS).astype(I32)
    used = (pend[-1] // MOE_ROWS).astype(I32)
    tail = jnp.stack([used, nb - used])

    xs = gather_rows(h_all, row_tok, BF16)
    act = moe_up(xs, w_up, b_up, layer, chunk_start, n_chunks, tail)
    rows = moe_down(act, w_down, b_down, layer, chunk_start, n_chunks, tail)
    return moe_combine(rows, dest.reshape(n, TOP_K).astype(I32), top_gate)


def _pad_rows_front(a, rows):
    pad = rows - a.shape[1]
    return jnp.pad(a, ((0, 0), (pad, 0), (0, 0)))


def _pad_lanes(a):
    return jnp.pad(a, [(0, 0)] * (a.ndim - 1) + [(0, LANES - a.shape[-1])])


def kernel(x_prompt, x_sample, c_prompt, c_sample, cache_pool, cache_k, cache_v, cache_conv, state_C, state_n, state_m, ab_w_in, ab_w_pool, ab_pool_scale, ab_w_out, ml_w_in, ml_conv_w, ml_conv_b, ml_b_i, ml_b_f, ml_w_out, ada_w, ada_b, ln_g, ln_b, router_w, router_b, moe_w_up, moe_b_up, moe_w_down, moe_b_down):
    bp, tp, d = x_prompt.shape
    bs, ts, _ = x_sample.shape
    depth = ada_w.shape[0]
    alpha = (2 * depth) ** 0.25
    past, sb_heads, sb_dim = cache_k.shape[2:]
    sb_w = sb_heads * sb_dim
    pw = cache_pool.shape[-1]
    ml_heads, dk, dv = state_C.shape[2:]
    qkw = 2 * ml_heads * dk
    vw = ml_heads * dv
    n_exp = router_w.shape[-1]
    mp, ms = bp * tp, bs * ts
    n_tok = mp + ms

    n_c = bp + bs
    c_rows = -(-n_c // 16) * 16
    c_all = jnp.pad(jnp.concatenate([c_prompt, c_sample], axis=0), ((0, c_rows - n_c), (0, 0)))
    mod = adaln(c_all, ada_w, ada_b).reshape(depth, c_rows, 6, d)

    def terms(layer, which):
        rows = slice(0, bp) if which == 0 else slice(bp, bp + bs)
        return [mod[layer, rows, i, :][:, None, :] for i in range(6)]

    router_wp = _pad_lanes(router_w)
    router_bp = _pad_lanes(router_b)[:, None, :]
    b_up4 = moe_b_up[:, :, None, :]
    b_down4 = moe_b_down[:, :, None, :]

    tq = min(512, tp // 2)
    tk_att = min(256, tq)
    tkn = LANES
    tri_n = max(tk_att, min(512, past), tkn)
    tri = (jnp.arange(tri_n)[:, None] > jnp.arange(tri_n)[None, :]).astype(BF16)

    xs_res = [x_prompt.reshape(mp, d), x_sample.reshape(ms, d)]
    trunk_rows = [tp, ts]
    trunk_b = [bp, bs]
    row_off = [0, mp]
    h_mix = [None, None]
    pools, ks, vs, convs, cs, ns, mstates = [[[], []] for _ in range(7)]

    for layer in range(depth):
        j = layer // 2
        t6 = [terms(layer, 0), terms(layer, 1)]
        if layer == 0:
            for w in range(2):
                sh_m, sc_m = t6[w][0], t6[w][1]
                h_mix[w] = modulate(xs_res[w].reshape(trunk_b[w], trunk_rows[w], d), sc_m, sh_m)
        y_mix = [None, None]
        if layer % 2 == 0:
            w_in = ab_w_in[j].astype(BF16)
            w_out = ab_w_out[j].astype(BF16)
            w_pool = ab_w_pool[j].astype(BF16)
            scale = ab_pool_scale[j][None, :]
            for w in range(2):
                bb, tt = trunk_b[w], trunk_rows[w]
                u, q, k, v = [matmul([h_mix[w]], w_in, [0], c0, wd)
                              for c0, wd in ((0, pw), (pw, sb_w), (pw + sb_w, sb_w), (pw + 2 * sb_w, sb_w))]
                u3 = u.reshape(bb, tt, pw)
                if w == 0:
                    hist = jnp.zeros((bb, POOL_HALO, pw), F32)
                    pool_out = pool_mix(u3, hist, w_pool, scale, 0)
                    att = attn_prompt(q.reshape(bb, tt, sb_w), k.reshape(bb, tt, sb_w), v.reshape(bb, tt, sb_w),
                                      tri[:tk_att, :tk_att], sb_heads, tq)
                    pools[w].append(u3[:, tt - (POOL_HALO - 1):])
                else:
                    hist = _pad_rows_front(cache_pool[j], POOL_HALO)
                    pool_out = pool_mix(u3, hist, w_pool, scale, past)
                    z_pool = jnp.concatenate([cache_pool[j], u3], axis=1)
                    pools[w].append(z_pool[:, -(POOL_HALO - 1):])
                    q4 = q.reshape(bb, tt, sb_heads, sb_dim)
                    eye = jnp.eye(sb_heads, dtype=F32)
                    q_bd = jnp.einsum('bihd,hg->bhigd', q4, eye).reshape(bb, sb_heads * tt, sb_w).astype(BF16)
                    k_new = jnp.pad(k.reshape(bb, tt, sb_w), ((0, 0), (0, tkn - tt), (0, 0)))
                    v_new = jnp.pad(v.reshape(bb, tt, sb_w), ((0, 0), (0, tkn - tt), (0, 0)))
                    o_bd = attn_sample(q_bd, k_new, v_new, cache_k[j].reshape(bb, past * sb_heads, sb_dim),
                                       cache_v[j].reshape(bb, past * sb_heads, sb_dim), tri, tt, sb_dim)
                    o5 = o_bd.reshape(bb, sb_heads, tt, sb_heads, sb_dim)
                    att = jnp.einsum('bhihd->bihd', o5).reshape(bb * tt, sb_w).astype(BF16)
                ks[w].append(k.reshape(bb, tt, sb_heads, sb_dim))
                vs[w].append(v.reshape(bb, tt, sb_heads, sb_dim))
                y_mix[w] = matmul([pool_out, att], w_out, [0, pw], 0, d)
        else:
            w_in = ml_w_in[j]
            o1 = qkw
            o2 = o1 + vw
            o3 = o2 + vw
            w_main = w_in[:, :o3].astype(BF16)
            w_gate = _pad_lanes(w_in[:, o3:]).astype(BF16)
            w_out = ml_w_out[j].astype(BF16)
            conv_w = jnp.pad(ml_conv_w[j], ((0, CONV_HALO - CONV_W), (0, 0)))
            conv_b = ml_conv_b[j][None, :]
            gate_b = _pad_lanes(jnp.concatenate([ml_b_i[j], ml_b_f[j]]))[None, :]
            for w in range(2):
                bb, tt = trunk_b[w], trunk_rows[w]
                qk_pre = matmul([h_mix[w]], w_main, [0], 0, qkw).reshape(bb, tt, qkw)
                v_in = matmul([h_mix[w]], w_main, [0], o1, vw).reshape(bb, tt, vw)
                o_pre = matmul([h_mix[w]], w_main, [0], o2, vw).reshape(bb, tt, vw)
                gates = matmul([h_mix[w]], w_gate, [0], 0, LANES, tn=LANES).reshape(bb, tt, LANES)
                if w == 0:
                    hist = jnp.zeros((bb, CONV_HALO, qkw), F32)
                    c0 = jnp.zeros((bb, ml_heads, dk, dv), F32)
                    n0 = jnp.zeros((bb, ml_heads, dk), F32)
                    m0 = jnp.zeros((bb, 1, LANES), F32)
                    chunk = min(256, tt)
                    hid, c1, n1, m1 = mlstm_mix(qk_pre, v_in, o_pre, gates, conv_w, conv_b, gate_b, hist,
                                                c0, n0, m0, chunk, chunk)
                    convs[w].append(qk_pre[:, tt - (CONV_W - 1):])
                else:
                    hist = _pad_rows_front(cache_conv[j], CONV_HALO)
                    m0 = _pad_lanes(state_m[j])[:, None, :]
                    chunk = -(-tt // LANES) * LANES
                    padt = lambda a: jnp.pad(a, ((0, 0), (0, chunk - tt), (0, 0)))
                    hid, c1, n1, m1 = mlstm_mix(padt(qk_pre), padt(v_in), padt(o_pre), padt(gates), conv_w, conv_b,
                                                gate_b, hist, state_C[j], state_n[j], m0, chunk, tt)
                    hid = hid[:, :tt]
                    z_conv = jnp.concatenate([cache_conv[j], qk_pre], axis=1)
                    convs[w].append(z_conv[:, -(CONV_W - 1):])
                cs[w].append(c1)
                ns[w].append(n1)
                mstates[w].append(m1[:, 0, :ml_heads])
                y_mix[w] = matmul([hid.reshape(bb * tt, vw)], w_out, [0], 0, d)

        lg1, lb1 = ln_g[layer, 0][None, :], ln_b[layer, 0][None, :]
        lg2, lb2 = ln_g[layer, 1][None, :], ln_b[layer, 1][None, :]
        router = (router_wp[layer], router_bp[layer], n_exp)
        shared = (jnp.zeros((n_tok, d), F32), jnp.zeros((n_tok, LANES), I32), jnp.zeros((n_tok, LANES), F32))
        x_mid = [None, None]
        for w in range(2):
            g_m, sh_f, sc_f = t6[w][2], t6[w][3], t6[w][4]
            x_mid[w], *shared = deepnorm(xs_res[w], y_mix[w], 0, g_m, lg1, lb1, alpha, trunk_rows[w],
                                         nxt=(sc_f, sh_f), router=router, dest=tuple(shared),
                                         dest_row_off=row_off[w], next_dtype=F32)
        h_all, top_idx, top_gate = shared
        y_ffn = moe_ffn(h_all, top_idx, top_gate, moe_w_up, b_up4, moe_w_down, b_down4, layer)
        for w in range(2):
            g_f = t6[w][5]
            if layer + 1 < depth:
                t_next = terms(layer + 1, w)
                xs_res[w], h_mix[w] = deepnorm(x_mid[w], y_ffn, row_off[w], g_f, lg2, lb2, alpha, trunk_rows[w],
                                               nxt=(t_next[1], t_next[0]))
            else:
                (xs_res[w],) = deepnorm(x_mid[w], y_ffn, row_off[w], g_f, lg2, lb2, alpha, trunk_rows[w])

    outs = []
    for w in range(2):
        outs.append((xs_res[w].reshape(trunk_b[w], trunk_rows[w], d), jnp.stack(pools[w]), jnp.stack(ks[w]),
                     jnp.stack(vs[w]), jnp.stack(convs[w]), jnp.stack(cs[w]), jnp.stack(ns[w]),
                     jnp.stack(mstates[w])))
    (y_p, *rest_p), (y_s, *rest_s) = outs
    return (y_p, y_s, *rest_p, *rest_s)
```

```python
import functools

import jax
import jax.numpy as jnp
from jax import lax
from jax.experimental import pallas as pl
from jax.experimental.pallas import tpu as pltpu

F32 = jnp.float32
BF16 = jnp.bfloat16
I32 = jnp.int32

TOP_K = 4
POOL_WINDOWS = (2, 4, 8, 16)
POOL_HALO = 16
CONV_W = 4
CONV_HALO = 8
SWIGLU_LIMIT = 7.0
SWIGLU_ALPHA = 1.702
LN_EPS = 1e-5
LANES = 128
MXU_COLS = 256
MOE_ROWS = 256
CHUNK_DMA_PRIORITY = 1
GATHER_ROWS = 256
COMBINE_TOKENS = 64
VMEM_LIMIT = 56 * 1024 * 1024
NEG_BIG = -1e30

_HIGHEST = lax.Precision.HIGHEST


def _params(sem, vmem=None):
    return pltpu.CompilerParams(dimension_semantics=sem, vmem_limit_bytes=vmem or VMEM_LIMIT)


def _log_sigmoid(z):
    return jnp.minimum(z, 0.0) - jnp.log(1.0 + jnp.exp(-jnp.abs(z)))


def _adaln_kernel(c_ref, w_ref, b_ref, o_ref):
    c = c_ref[...]
    s = (c * jax.nn.sigmoid(c)).astype(BF16)
    o_ref[...] = jnp.dot(s, w_ref[...].astype(BF16), preferred_element_type=F32) + b_ref[...]


def adaln(c_all, ada_w, ada_b):
    nl, d, n = ada_w.shape
    c = c_all.shape[0]
    tn = min(1024, n)
    return pl.pallas_call(
        _adaln_kernel,
        grid=(nl, n // tn),
        in_specs=[pl.BlockSpec((c, d), lambda l, j: (0, 0)),
                  pl.BlockSpec((None, d, tn), lambda l, j: (l, 0, j)),
                  pl.BlockSpec((None, 1, tn), lambda l, j: (l, 0, j))],
        out_specs=pl.BlockSpec((None, c, tn), lambda l, j: (l, 0, j)),
        out_shape=jax.ShapeDtypeStruct((nl, c, n), F32),
        compiler_params=_params(("arbitrary", "arbitrary")),
        name="adaln",
    )(c_all, ada_w, ada_b.reshape(nl, 1, n))


def _modulate_kernel(x_ref, sc_ref, sh_ref, o_ref):
    o_ref[...] = (x_ref[...] * (1.0 + sc_ref[...]) + sh_ref[...]).astype(o_ref.dtype)


def modulate(x, sc, sh):
    g, r, d = x.shape
    tr = min(512, r)
    out = pl.pallas_call(
        _modulate_kernel,
        grid=(g, r // tr),
        in_specs=[pl.BlockSpec((None, tr, d), lambda b, i: (b, i, 0)),
                  pl.BlockSpec((None, 1, d), lambda b, i: (b, 0, 0)),
                  pl.BlockSpec((None, 1, d), lambda b, i: (b, 0, 0))],
        out_specs=pl.BlockSpec((None, tr, d), lambda b, i: (b, i, 0)),
        out_shape=jax.ShapeDtypeStruct((g, r, d), BF16),
        compiler_params=_params(("arbitrary", "arbitrary")),
        name="modulate",
    )(x, sc, sh)
    return out.reshape(g * r, d)


def _matmul_kernel(*refs, n_a):
    a_refs, w_refs, o_ref = refs[:n_a], refs[n_a:2 * n_a], refs[2 * n_a]
    acc = jnp.dot(a_refs[0][...], w_refs[0][...], preferred_element_type=F32)
    for a, w in zip(a_refs[1:], w_refs[1:]):
        acc = acc + jnp.dot(a[...], w[...], preferred_element_type=F32)
    o_ref[...] = acc.astype(o_ref.dtype)


def matmul(a_list, w, row_offs, col_off, n_cols, out_dtype=F32, tm=512, tn=1024):
    m = a_list[0].shape[0]
    tm = min(tm, m)
    tn = min(tn, n_cols)
    assert m % tm == 0 and n_cols % tn == 0 and col_off % tn == 0
    n_a = len(a_list)
    in_specs = [pl.BlockSpec((tm, a.shape[1]), lambda j, i: (i, 0)) for a in a_list]
    for a, ro in zip(a_list, row_offs):
        ka = a.shape[1]
        assert ro % ka == 0
        in_specs.append(pl.BlockSpec((ka, tn), functools.partial(
            lambda j, i, rb, cb: (rb, cb + j), rb=ro // ka, cb=col_off // tn)))
    return pl.pallas_call(
        functools.partial(_matmul_kernel, n_a=n_a),
        grid=(n_cols // tn, m // tm),
        in_specs=in_specs,
        out_specs=pl.BlockSpec((tm, tn), lambda j, i: (i, j)),
        out_shape=jax.ShapeDtypeStruct((m, n_cols), out_dtype),
        compiler_params=_params(("arbitrary", "arbitrary")),
        name="matmul",
    )(*a_list, *([w] * n_a))


def _pool_kernel(u_ref, hist_ref, w_ref, scale_ref, o_ref, z_ref, *, tt, pg, pos0):
    ti = pl.program_id(1)

    @pl.when(ti == 0)
    def _():
        z_ref[0:POOL_HALO, :] = hist_ref[...]

    @pl.when(ti > 0)
    def _():
        z_ref[0:POOL_HALO, :] = z_ref[tt:tt + POOL_HALO, :]

    z_ref[POOL_HALO:POOL_HALO + tt, :] = u_ref[...]
    pos = pos0 + ti * tt + lax.broadcasted_iota(I32, (tt, 1), 0)
    for g, w in enumerate(POOL_WINDOWS):
        cols = slice(g * pg, (g + 1) * pg)
        u_g = z_ref[POOL_HALO:POOL_HALO + tt, cols]
        acc = u_g
        for j in range(1, w):
            acc = acc + z_ref[POOL_HALO - j:POOL_HALO - j + tt, cols]
        cnt = jnp.minimum(w, pos + 1).astype(F32)
        diff = (acc / cnt - u_g).astype(BF16)
        y = jnp.dot(diff, w_ref[g], preferred_element_type=F32) * scale_ref[:, cols]
        o_ref[:, cols] = y.astype(o_ref.dtype)


def pool_mix(u, hist, w_pool, scale, pos0):
    b, t, pw = u.shape
    ng, pg, _ = w_pool.shape
    tt = min(512, t)
    assert tt >= POOL_HALO and t % tt == 0 and ng == len(POOL_WINDOWS)
    out = pl.pallas_call(
        functools.partial(_pool_kernel, tt=tt, pg=pg, pos0=pos0),
        grid=(b, t // tt),
        in_specs=[pl.BlockSpec((None, tt, pw), lambda i, j: (i, j, 0)),
                  pl.BlockSpec((None, POOL_HALO, pw), lambda i, j: (i, 0, 0)),
                  pl.BlockSpec((ng, pg, pg), lambda i, j: (0, 0, 0)),
                  pl.BlockSpec((1, pw), lambda i, j: (0, 0))],
        out_specs=pl.BlockSpec((None, tt, pw), lambda i, j: (i, j, 0)),
        out_shape=jax.ShapeDtypeStruct((b, t, pw), BF16),
        scratch_shapes=[pltpu.VMEM((tt + POOL_HALO, pw), F32)],
        compiler_params=_params(("arbitrary", "arbitrary")),
        name="pool_mix",
    )(u, hist, w_pool, scale)
    return out.reshape(b * t, pw)


def _sb_block(z, lsz_mask, tri, v_bf, r_prev):
    lsz, lk, after = _sb_after(z, lsz_mask, tri)
    return _sb_weighted(lsz, after, lsz_mask, v_bf, r_prev), jnp.sum(lk, axis=1, keepdims=True)


def _sb_after(z, mask, tri):
    lsz = _log_sigmoid(z)
    lk = lsz - z
    if mask is not None:
        lk = jnp.where(mask, lk, 0.0)
    hi = lk.astype(BF16)
    lo = (lk - hi.astype(F32)).astype(BF16)
    after = (jnp.dot(hi, tri, preferred_element_type=F32)
             + jnp.dot(lo, tri, preferred_element_type=F32))
    return lsz, lk, after


def _sb_weighted(lsz, after, mask, v_bf, r_prev):
    a = jnp.exp(lsz + after + r_prev)
    if mask is not None:
        a = jnp.where(mask, a, 0.0)
    return jnp.dot(a.astype(BF16), v_bf, preferred_element_type=F32)


def _attn_prompt_kernel(q_ref, k_ref, v_ref, tri_ref, o_ref, kb_ref, vb_ref, r_ref, acc_ref, *, scale, tq, tk):
    qi = pl.program_id(2)

    @pl.when(qi == 0)
    def _():
        kb_ref[...] = k_ref[...].astype(BF16)
        vb_ref[...] = v_ref[...].astype(BF16)

    q = q_ref[...].astype(BF16)
    r_ref[...] = jnp.zeros_like(r_ref)
    acc_ref[...] = jnp.zeros_like(acc_ref)

    def key_block(k0, masked):
        subs = list(reversed(range(tq // tk)))
        starts = [pl.multiple_of(k0 + sub * tk, tk) for sub in subs]
        masks = [None] * len(subs)
        if masked:
            row = lax.broadcasted_iota(I32, (tq, tk), 0)
            masks = [lax.broadcasted_iota(I32, (tq, tk), 1) + sub * tk < row for sub in subs]
        zs = [lax.dot_general(q, kb_ref[pl.ds(ks, tk), :], (((1,), (1,)), ((), ())),
                              preferred_element_type=F32) * scale for ks in starts]
        stages = [_sb_after(z, m, tri_ref[...]) for z, m in zip(zs, masks)]
        r = r_ref[...]
        acc = acc_ref[...]
        for (lsz, lk, after), m, ks in zip(stages, masks, starts):
            acc = acc + _sb_weighted(lsz, after, m, vb_ref[pl.ds(ks, tk), :], r)
            r = r + jnp.sum(lk, axis=1, keepdims=True)
        acc_ref[...] = acc
        r_ref[...] = r

    key_block(qi * tq, True)

    def body(i, carry):
        key_block((qi - 1 - i) * tq, False)
        return carry

    lax.fori_loop(0, qi, body, 0)
    o_ref[...] = acc_ref[...].astype(o_ref.dtype)


def attn_prompt(q, k, v, tri, heads, tq):
    b, t, w = q.shape
    d = w // heads
    tk = tri.shape[0]
    assert t % tq == 0 and tq % tk == 0
    kv_spec = pl.BlockSpec((None, t, d), lambda bi, h, qi: (bi, 0, h))
    out = pl.pallas_call(
        functools.partial(_attn_prompt_kernel, scale=d ** -0.5, tq=tq, tk=tk),
        grid=(b, heads, t // tq),
        in_specs=[pl.BlockSpec((None, tq, d), lambda bi, h, qi: (bi, qi, h)),
                  kv_spec, kv_spec,
                  pl.BlockSpec((tk, tk), lambda bi, h, qi: (0, 0))],
        out_specs=pl.BlockSpec((None, tq, d), lambda bi, h, qi: (bi, qi, h)),
        out_shape=jax.ShapeDtypeStruct((b, t, w), BF16),
        scratch_shapes=[pltpu.VMEM((t, d), BF16), pltpu.VMEM((t, d), BF16),
                        pltpu.VMEM((tq, 1), F32), pltpu.VMEM((tq, d), F32)],
        compiler_params=_params(("arbitrary",) * 3),
        name="attn_prompt",
    )(q, k, v, tri)
    return out.reshape(b * t, w)


def _attn_sample_kernel(q_ref, kn_ref, vn_ref, kp_ref, vp_ref, tri_ref, o_ref, r_ref, *, scale, ts, tkn, tkp, heads):
    j = pl.program_id(1)
    rows = q_ref.shape[0]

    def block(k, v, tk, masked):
        z = lax.dot_general(q_ref[...], k, (((1,), (1,)), ((), ())), preferred_element_type=F32) * scale
        mask = None
        if masked:
            row = lax.rem(lax.broadcasted_iota(I32, (rows, tk), 0), ts)
            col = lax.broadcasted_iota(I32, (rows, tk), 1)
            mask = col < row
        pv, lk_sum = _sb_block(z, mask, tri_ref[0:tk, 0:tk], v, r_ref[...])
        o_ref[...] += pv
        r_ref[...] += lk_sum

    def heads_on_lanes(ref):
        return jnp.concatenate([ref[pl.ds(h, tkp, stride=heads), :].astype(BF16) for h in range(heads)], axis=1)

    @pl.when(j == 0)
    def _():
        r_ref[...] = jnp.zeros_like(r_ref)
        o_ref[...] = jnp.zeros_like(o_ref)
        block(kn_ref[...].astype(BF16), vn_ref[...].astype(BF16), tkn, True)

    @pl.when(j > 0)
    def _():
        block(heads_on_lanes(kp_ref), heads_on_lanes(vp_ref), tkp, False)


def attn_sample(q_bd, k_new, v_new, k_past, v_past, tri, ts, d):
    b, rows, w = q_bd.shape
    heads = w // d
    tkn = k_new.shape[1]
    past = k_past.shape[1] // heads
    tkp = min(512, past)
    assert past % tkp == 0 and tri.shape[0] >= max(tkn, tkp)
    n_past = past // tkp
    past_spec = pl.BlockSpec((None, tkp * heads, d), lambda bi, j: (bi, n_past - jnp.maximum(j, 1), 0))
    new_spec = pl.BlockSpec((None, tkn, w), lambda bi, j: (bi, 0, 0))
    return pl.pallas_call(
        functools.partial(_attn_sample_kernel, scale=d ** -0.5, ts=ts, tkn=tkn, tkp=tkp, heads=heads),
        grid=(b, 1 + n_past),
        in_specs=[pl.BlockSpec((None, rows, w), lambda bi, j: (bi, 0, 0)),
                  new_spec, new_spec, past_spec, past_spec,
                  pl.BlockSpec(tri.shape, lambda bi, j: (0, 0))],
        out_specs=pl.BlockSpec((None, rows, w), lambda bi, j: (bi, 0, 0)),
        out_shape=jax.ShapeDtypeStruct((b, rows, w), F32),
        scratch_shapes=[pltpu.VMEM((rows, 1), F32)],
        compiler_params=_params(("arbitrary", "arbitrary")),
        name="attn_sample",
    )(q_bd, k_new, v_new, k_past, v_past, tri)


def _mlstm_kernel(qk_ref, v_ref, op_ref, g_ref, cw_ref, cb_ref, gb_ref, hist_ref, c0_ref, n0_ref, m0_ref,
                  tril_ref, triu_ref,
                  hid_ref, c1_ref, n1_ref, m1_ref,
                  z_ref, c_ref, n_ref, m_ref, *, L, H, Dk, Dv, n_valid, kscale):
    c = pl.program_id(1)
    qw = H * Dk

    @pl.when(c == 0)
    def _():
        z_ref[0:CONV_HALO, :] = hist_ref[...]
        c_ref[...] = c0_ref[...]
        n_ref[...] = n0_ref[...]
        m_ref[...] = m0_ref[...]

    @pl.when(c > 0)
    def _():
        z_ref[0:CONV_HALO, :] = z_ref[L:L + CONV_HALO, :]

    z_ref[CONV_HALO:CONV_HALO + L, :] = qk_ref[...]
    y = cb_ref[...]
    for j in range(CONV_W):
        r0 = CONV_HALO - (CONV_W - 1) + j
        y = y + z_ref[r0:r0 + L, :] * cw_ref[j:j + 1, :]
    qk = y * jax.nn.sigmoid(y)

    g = g_ref[...] + gb_ref[...]
    lane = lax.broadcasted_iota(I32, (L, LANES), 1)
    gl = jnp.where(jnp.logical_and(lane >= H, lane < 2 * H), _log_sigmoid(g), g)
    if n_valid < L:
        row = lax.broadcasted_iota(I32, (L, LANES), 0)
        gl = jnp.where(row < n_valid, gl, jnp.where(lane < H, NEG_BIG, 0.0))
    glt = gl.T
    b_col = jnp.dot(tril_ref[...], gl, preferred_element_type=F32, precision=_HIGHEST)
    b_row = jnp.dot(glt, triu_ref[...], preferred_element_type=F32, precision=_HIGHEST)
    trow = lax.broadcasted_iota(I32, (L, L), 0)
    tcol = lax.broadcasted_iota(I32, (L, L), 1)
    causal = tcol <= trow

    for h in range(H):
        m_prev = m_ref[:, h:h + 1]
        bc = b_col[:, H + h:H + h + 1]
        br = b_row[H + h:H + h + 1, :]
        igc = gl[:, h:h + 1]
        igr = glt[h:h + 1, :]
        dm = jnp.where(causal, bc - br + igr, -jnp.inf)
        inter = bc + m_prev
        m_t = jnp.maximum(inter, jnp.max(dm, axis=1, keepdims=True))
        qh = qk[:, h * Dk:(h + 1) * Dk]
        kh = qk[:, qw + h * Dk:qw + (h + 1) * Dk] * kscale
        qb = qh.astype(BF16)
        vb = v_ref[:, h * Dv:(h + 1) * Dv].astype(BF16)
        s = jnp.exp(dm - m_t) * lax.dot_general(qb, kh.astype(BF16), (((1,), (1,)), ((), ())),
                                                 preferred_element_type=F32)
        decay = jnp.exp(inter - m_t)
        ch = c_ref[h]
        nh = n_ref[h:h + 1, :]
        num = (jnp.dot(s.astype(BF16), vb, preferred_element_type=F32)
               + decay * jnp.dot(qb, ch.astype(BF16), preferred_element_type=F32))
        den = jnp.sum(s, axis=1, keepdims=True) + decay * jnp.sum(qh * nh, axis=1, keepdims=True)
        hval = num / jnp.maximum(jnp.abs(den), jnp.exp(-m_t))
        og = jax.nn.sigmoid(op_ref[:, h * Dv:(h + 1) * Dv])
        hid_ref[:, h * Dv:(h + 1) * Dv] = (og * hval).astype(hid_ref.dtype)

        b_last = bc[L - 1:L, :]
        gc = b_last - bc + igc
        m_new = jnp.maximum(b_last + m_prev, jnp.max(gc, axis=0, keepdims=True))
        kw = kh * jnp.exp(gc - m_new)
        keep = jnp.exp(b_last + m_prev - m_new)
        c_ref[h] = keep * ch + jnp.dot(kw.T.astype(BF16), vb, preferred_element_type=F32)
        n_ref[h:h + 1, :] = keep * nh + jnp.sum(kw, axis=0, keepdims=True)
        m_ref[:, h:h + 1] = m_new

    @pl.when(c == pl.num_programs(1) - 1)
    def _():
        c1_ref[...] = c_ref[...]
        n1_ref[...] = n_ref[...]
        m1_ref[...] = m_ref[...]


def mlstm_mix(qk_pre, v, o_pre, gates, conv_w, conv_b, gate_b, hist, c0, n0, m0, L, n_valid):
    b, t, qk2 = qk_pre.shape
    _, hh, dk, dv = c0.shape
    vw = hh * dv
    assert t % L == 0
    tril = jnp.tril(jnp.ones((L, L), F32))
    row3 = lambda i, j: (i, j, 0)
    fix3 = lambda i, j: (i, 0, 0)
    fix2 = lambda i, j: (0, 0)
    return pl.pallas_call(
        functools.partial(_mlstm_kernel, L=L, H=hh, Dk=dk, Dv=dv, n_valid=n_valid, kscale=dk ** -0.5),
        grid=(b, t // L),
        in_specs=[pl.BlockSpec((None, L, qk2), row3),
                  pl.BlockSpec((None, L, vw), row3),
                  pl.BlockSpec((None, L, vw), row3),
                  pl.BlockSpec((None, L, LANES), row3),
                  pl.BlockSpec(conv_w.shape, fix2),
                  pl.BlockSpec(conv_b.shape, fix2),
                  pl.BlockSpec(gate_b.shape, fix2),
                  pl.BlockSpec((None, CONV_HALO, qk2), fix3),
                  pl.BlockSpec((None, hh, dk, dv), lambda i, j: (i, 0, 0, 0)),
                  pl.BlockSpec((None, hh, dk), fix3),
                  pl.BlockSpec((None, 1, LANES), fix3),
                  pl.BlockSpec((L, L), fix2),
                  pl.BlockSpec((L, L), fix2)],
        out_specs=[pl.BlockSpec((None, L, vw), row3),
                   pl.BlockSpec((None, hh, dk, dv), lambda i, j: (i, 0, 0, 0)),
                   pl.BlockSpec((None, hh, dk), fix3),
                   pl.BlockSpec((None, 1, LANES), fix3)],
        out_shape=[jax.ShapeDtypeStruct((b, t, vw), BF16),
                   jax.ShapeDtypeStruct(c0.shape, F32),
                   jax.ShapeDtypeStruct(n0.shape, F32),
                   jax.ShapeDtypeStruct(m0.shape, F32)],
        scratch_shapes=[pltpu.VMEM((L + CONV_HALO, qk2), F32),
                        pltpu.VMEM((hh, dk, dv), F32),
                        pltpu.VMEM((hh, dk), F32),
                        pltpu.VMEM((1, LANES), F32)],
        compiler_params=_params(("arbitrary", "arbitrary")),
        name="mlstm_mix",
    )(qk_pre, v, o_pre, gates, conv_w, conv_b, gate_b, hist, c0, n0, m0, tril, tril.T)


def _ln_kernel(*refs, alpha, with_next, with_router, n_experts, next_dtype):
    x_ref, y_ref, gate_ref, lg_ref, lb_ref = refs[:5]
    pos = 5
    if with_next:
        sc_ref, sh_ref = refs[pos:pos + 2]
        pos += 2
    if with_router:
        rw_ref, rb_ref = refs[pos:pos + 2]
        pos += 2
    n_out = 1 + int(with_next) + 2 * int(with_router)
    outs = refs[len(refs) - n_out:]
    xo_ref = outs[0]

    v = alpha * x_ref[...] + (1.0 + gate_ref[...]) * y_ref[...]
    mu = jnp.mean(v, axis=-1, keepdims=True)
    vc = v - mu
    var = jnp.mean(vc * vc, axis=-1, keepdims=True)
    xn = vc * lax.rsqrt(var + LN_EPS) * lg_ref[...] + lb_ref[...]
    xo_ref[...] = xn
    if not with_next:
        return
    h = xn * (1.0 + sc_ref[...]) + sh_ref[...]
    outs[1][...] = h.astype(next_dtype)
    if not with_router:
        return
    tm = h.shape[0]
    logits = jnp.dot(h.astype(BF16), rw_ref[...].astype(BF16), preferred_element_type=F32) + rb_ref[...]
    lane = lax.broadcasted_iota(I32, (tm, LANES), 1).astype(F32)
    logits = jnp.where(lane < n_experts, logits, -jnp.inf)
    idx_out = jnp.zeros((tm, LANES), F32)
    gate_out = jnp.zeros((tm, LANES), F32)
    top0 = None
    for k in range(TOP_K):
        mk = jnp.max(logits, axis=1, keepdims=True)
        ik = jnp.min(jnp.where(logits == mk, lane, float(LANES)), axis=1, keepdims=True)
        logits = jnp.where(lane == ik, -jnp.inf, logits)
        if top0 is None:
            top0 = mk
        idx_out = jnp.where(lane == k, ik, idx_out)
        gate_out = jnp.where(lane == k, jnp.exp(mk - top0), gate_out)
    gate_out = gate_out / jnp.sum(gate_out, axis=1, keepdims=True)
    outs[2][...] = idx_out.astype(I32)
    outs[3][...] = gate_out


def deepnorm(x, y, y_row_off, gate, ln_g, ln_b, alpha, rows_per_group, nxt=None, router=None,
             dest=None, dest_row_off=0, next_dtype=BF16):
    m, d = x.shape
    tm = min(256, rows_per_group)
    assert rows_per_group % tm == 0 and y_row_off % tm == 0 and dest_row_off % tm == 0
    bpg = rows_per_group // tm
    yo = y_row_off // tm
    do = dest_row_off // tm
    row = lambda i: (i, 0)
    grp = lambda i: (i // bpg, 0, 0)
    fix = lambda i: (0, 0)
    args = [x, y, gate, ln_g, ln_b]
    in_specs = [pl.BlockSpec((tm, d), row), pl.BlockSpec((tm, d), lambda i: (i + yo, 0)),
                pl.BlockSpec((None, 1, d), grp), pl.BlockSpec((1, d), fix), pl.BlockSpec((1, d), fix)]
    out_shape = [jax.ShapeDtypeStruct((m, d), F32)]
    out_specs = [pl.BlockSpec((tm, d), row)]
    n_experts = 0
    if nxt is not None:
        args += list(nxt)
        in_specs += [pl.BlockSpec((None, 1, d), grp)] * 2
    if router is not None:
        rw, rb, n_experts = router
        args += [rw, rb]
        in_specs += [pl.BlockSpec(rw.shape, fix), pl.BlockSpec(rb.shape, fix)]
    aliases = {}
    dst = lambda i: (i + do, 0)
    if nxt is not None:
        widths = [(d, next_dtype)] + ([(LANES, I32), (LANES, F32)] if router is not None else [])
        for k, (wd, dt) in enumerate(widths):
            if dest is not None:
                aliases[len(args)] = 1 + k
                args.append(dest[k])
                in_specs.append(pl.BlockSpec(memory_space=pl.ANY))
                out_shape.append(jax.ShapeDtypeStruct(dest[k].shape, dt))
            else:
                out_shape.append(jax.ShapeDtypeStruct((m, wd), dt))
            out_specs.append(pl.BlockSpec((tm, wd), dst))
    return pl.pallas_call(
        functools.partial(_ln_kernel, alpha=alpha, with_next=nxt is not None, with_router=router is not None,
                          n_experts=n_experts, next_dtype=next_dtype),
        grid=(m // tm,),
        in_specs=in_specs,
        out_specs=out_specs,
        out_shape=out_shape,
        input_output_aliases=aliases,
        compiler_params=_params(("arbitrary",)),
        name="deepnorm",
    )(*args)


def _gather_kernel(idx_ref, idx_next_ref, src_ref, o_ref, buf_ref, sem, *, rb):
    i = pl.program_id(0)
    slot = lax.rem(i, 2)

    def copy(sl, r, src_row):
        return pltpu.make_async_copy(src_ref.at[pl.ds(src_row, 1)], buf_ref.at[sl, pl.ds(r, 1)], sem.at[sl])

    def start_block(rows_ref, sl):
        for r in range(rb):
            copy(sl, r, rows_ref[0, r]).start(priority=r % 2)

    @pl.when(i == 0)
    def _():
        start_block(idx_ref, 0)

    @pl.when(i + 1 < pl.num_programs(0))
    def _():
        start_block(idx_next_ref, 1 - slot)

    for r in range(rb):
        copy(slot, r, 0).wait()
    o_ref[...] = buf_ref[slot].astype(o_ref.dtype)


def gather_rows(src, idx, out_dtype):
    p = idx.shape[0]
    d = src.shape[1]
    rb = GATHER_ROWS
    assert p % rb == 0
    nblk = p // rb
    idx3 = idx.reshape(nblk, 1, rb)
    return pl.pallas_call(
        functools.partial(_gather_kernel, rb=rb),
        grid=(nblk,),
        in_specs=[pl.BlockSpec((None, 1, rb), lambda i: (i, 0, 0), memory_space=pltpu.SMEM),
                  pl.BlockSpec((None, 1, rb), lambda i: (jnp.minimum(i + 1, nblk - 1), 0, 0),
                               memory_space=pltpu.SMEM),
                  pl.BlockSpec(memory_space=pl.ANY)],
        out_specs=pl.BlockSpec((rb, d), lambda i: (i, 0)),
        out_shape=jax.ShapeDtypeStruct((p, d), out_dtype),
        scratch_shapes=[pltpu.VMEM((2, rb, d), src.dtype), pltpu.SemaphoreType.DMA((2,))],
        compiler_params=_params(("arbitrary",)),
        name="moe_gather",
    )(idx3, idx3, src)


def _expert_chunk_pairs(n, in_copy, out_copy, compute, first_step, prefetch_next):
    m = (n + 1) // 2

    def start_inputs(it, slot):
        in_copy(2 * it, slot, 0).start(priority=CHUNK_DMA_PRIORITY)

        @pl.when(2 * it + 1 < n)
        def _():
            in_copy(2 * it + 1, slot, 1).start(priority=CHUNK_DMA_PRIORITY)

    @pl.when(jnp.logical_and(first_step, n > 0))
    def _():
        start_inputs(0, 0)

    def body(it, carry):
        slot = lax.rem(it, 2)
        pair = 2 * it + 1 < n
        in_copy(2 * it, slot, 0).wait()

        @pl.when(pair)
        def _():
            in_copy(2 * it + 1, slot, 1).wait()

        @pl.when(it + 1 < m)
        def _():
            start_inputs(it + 1, 1 - slot)

        @pl.when(it >= 2)
        def _():
            out_copy(2 * it - 4, slot, 0).wait()
            out_copy(2 * it - 3, slot, 1).wait()

        @pl.when(pair)
        def _():
            compute(slot, True)
            out_copy(2 * it, slot, 0).start(priority=CHUNK_DMA_PRIORITY)
            out_copy(2 * it + 1, slot, 1).start(priority=CHUNK_DMA_PRIORITY)

        @pl.when(jnp.logical_not(pair))
        def _():
            compute(slot, False)
            out_copy(2 * it, slot, 0).start(priority=CHUNK_DMA_PRIORITY)

        return carry

    lax.fori_loop(0, m, body, 0)
    prefetch_next()

    @pl.when(m >= 2)
    def _():
        out_copy(2 * m - 4, lax.rem(m, 2), 0).wait()
        out_copy(2 * m - 3, lax.rem(m, 2), 1).wait()

    @pl.when(m >= 1)
    def _():
        out_copy(2 * m - 2, lax.rem(m + 1, 2), 0).wait()

        @pl.when(2 * m - 1 < n)
        def _():
            out_copy(2 * m - 1, lax.rem(m + 1, 2), 1).wait()


def _zero_tail(tail_ref, buf, out_copy_abs):
    buf[0] = jnp.zeros(buf.shape[1:], buf.dtype)
    t0, nt = tail_ref[0], tail_ref[1]

    def start(i, carry):
        out_copy_abs(t0 + i, 0).start()
        return carry

    def wait(i, carry):
        out_copy_abs(t0 + i, 0).wait()
        return carry

    lax.fori_loop(0, nt, start, 0)
    lax.fori_loop(0, nt, wait, 0)


def _moe_up_kernel(cs_ref, nc_ref, tail_ref, x_hbm, wg_ref, wl_ref, bg_ref, bl_ref, act_hbm,
                   wgb_ref, wlb_ref, xbuf, obuf, in_sem, out_sem):
    j = pl.program_id(0)
    e = pl.program_id(1)
    c0 = cs_ref[e]
    tn = wgb_ref.shape[1]
    wgb_ref[...] = wg_ref[...].astype(BF16)
    wlb_ref[...] = wl_ref[...].astype(BF16)

    def rows_of(c):
        return pl.ds(pl.multiple_of(c * MOE_ROWS, MOE_ROWS), MOE_ROWS)

    def half_of(h):
        return pl.ds(h * MOE_ROWS, MOE_ROWS)

    def in_copy_abs(c, slot, half=0):
        return pltpu.make_async_copy(x_hbm.at[rows_of(c)], xbuf.at[slot, half_of(half)], in_sem.at[slot])

    def out_copy_abs(c, slot, half=0):
        return pltpu.make_async_copy(obuf.at[slot, half_of(half)], act_hbm.at[j, rows_of(c)], out_sem.at[slot])

    def compute(slot, pair):
        rows = slice(0, 2 * MOE_ROWS if pair else MOE_ROWS)
        x = xbuf[slot, rows]
        for k0 in range(0, tn, MXU_COLS):
            cols = slice(k0, min(k0 + MXU_COLS, tn))
            hg = jnp.dot(x, wgb_ref[:, cols], preferred_element_type=F32) + bg_ref[:, cols]
            hl = jnp.dot(x, wlb_ref[:, cols], preferred_element_type=F32) + bl_ref[:, cols]
            glu = jnp.minimum(hg, SWIGLU_LIMIT)
            lin = jnp.clip(hl, -SWIGLU_LIMIT, SWIGLU_LIMIT)
            act = glu * jax.nn.sigmoid(SWIGLU_ALPHA * glu) * (lin + 1.0)
            obuf[slot, rows, cols] = act.astype(obuf.dtype)

    n_exp = pl.num_programs(1)
    step = j * n_exp + e
    e_next = jnp.where(e + 1 < n_exp, e + 1, 0)

    def prefetch_next():
        @pl.when(jnp.logical_and(step + 1 < pl.num_programs(0) * n_exp, nc_ref[e_next] > 0))
        def _():
            in_copy_abs(cs_ref[e_next], 0, 0).start(priority=CHUNK_DMA_PRIORITY)

            @pl.when(nc_ref[e_next] > 1)
            def _():
                in_copy_abs(cs_ref[e_next] + 1, 0, 1).start(priority=CHUNK_DMA_PRIORITY)

    _expert_chunk_pairs(nc_ref[e], lambda c, s, h: in_copy_abs(c0 + c, s, h),
                        lambda c, s, h: out_copy_abs(c0 + c, s, h), compute, step == 0, prefetch_next)

    @pl.when(e == n_exp - 1)
    def _():
        _zero_tail(tail_ref, obuf, out_copy_abs)


def moe_up(xs, w_up, b_up, layer, chunk_start, n_chunks, tail):
    p, d = xs.shape
    n_exp = w_up.shape[1]
    f = w_up.shape[3] // 2
    tn = min(1024, f)
    nj = f // tn
    any_spec = pl.BlockSpec(memory_space=pl.ANY)
    grid_spec = pltpu.PrefetchScalarGridSpec(
        num_scalar_prefetch=3,
        grid=(nj, n_exp),
        in_specs=[any_spec,
                  pl.BlockSpec((None, None, d, tn), lambda j, e, *_: (layer, e, 0, j)),
                  pl.BlockSpec((None, None, d, tn), lambda j, e, *_: (layer, e, 0, nj + j)),
                  pl.BlockSpec((None, None, 1, tn), lambda j, e, *_: (layer, e, 0, j)),
                  pl.BlockSpec((None, None, 1, tn), lambda j, e, *_: (layer, e, 0, nj + j))],
        out_specs=any_spec,
        scratch_shapes=[pltpu.VMEM((d, tn), BF16), pltpu.VMEM((d, tn), BF16),
                        pltpu.VMEM((2, 2 * MOE_ROWS, d), BF16), pltpu.VMEM((2, 2 * MOE_ROWS, tn), BF16),
                        pltpu.SemaphoreType.DMA((2,)), pltpu.SemaphoreType.DMA((2,))])
    return pl.pallas_call(
        _moe_up_kernel,
        grid_spec=grid_spec,
        out_shape=jax.ShapeDtypeStruct((nj, p, tn), BF16),
        compiler_params=_params(("arbitrary", "arbitrary")),
        name="moe_up",
    )(chunk_start, n_chunks, tail, xs, w_up, w_up, b_up, b_up)


class _ColumnTileCopy:
    def __init__(self, make, j, nj):
        self.make, self.j, self.nj = make, j, nj

    def start(self, priority=0):
        for jj in range(self.nj):
            pl.when(self.j == jj)(functools.partial(lambda jj: self.make(jj).start(priority=priority), jj))

    def wait(self):
        self.make(0).wait()


def _moe_down_kernel(cs_ref, nc_ref, tail_ref, a_hbm, w_ref, b_ref, rows_hbm, wb_ref, abuf, obuf, in_sem, out_sem):
    j = pl.program_id(0)
    e = pl.program_id(1)
    nj = pl.num_programs(0)
    c0 = cs_ref[e]
    nk, _, tk = a_hbm.shape
    tn = wb_ref.shape[1]
    wb_ref[...] = w_ref[...].astype(BF16)

    def rows_of(c):
        return pl.ds(pl.multiple_of(c * MOE_ROWS, MOE_ROWS), MOE_ROWS)

    def half_of(h):
        return pl.ds(h * MOE_ROWS, MOE_ROWS)

    def in_copy_abs(c, slot, half=0):
        return pltpu.make_async_copy(a_hbm.at[:, rows_of(c)], abuf.at[slot, :, half_of(half)], in_sem.at[slot])

    def out_copy_abs(c, slot, half=0):
        return _ColumnTileCopy(
            lambda jj: pltpu.make_async_copy(obuf.at[slot, half_of(half)],
                                             rows_hbm.at[rows_of(c), pl.ds(jj * tn, tn)], out_sem.at[slot]),
            j, rows_hbm.shape[1] // tn)

    def compute(slot, pair):
        rows = slice(0, 2 * MOE_ROWS if pair else MOE_ROWS)
        for k0 in range(0, tn, MXU_COLS):
            cols = slice(k0, min(k0 + MXU_COLS, tn))
            acc = b_ref[:, cols]
            for kk in range(nk):
                acc = acc + jnp.dot(abuf[slot, kk, rows], wb_ref[kk * tk:(kk + 1) * tk, cols],
                                    preferred_element_type=F32)
            obuf[slot, rows, cols] = acc

    n_exp = pl.num_programs(1)
    step = j * n_exp + e
    e_next = jnp.where(e + 1 < n_exp, e + 1, 0)

    def prefetch_next():
        @pl.when(jnp.logical_and(step + 1 < nj * n_exp, nc_ref[e_next] > 0))
        def _():
            in_copy_abs(cs_ref[e_next], 0, 0).start(priority=CHUNK_DMA_PRIORITY)

            @pl.when(nc_ref[e_next] > 1)
            def _():
                in_copy_abs(cs_ref[e_next] + 1, 0, 1).start(priority=CHUNK_DMA_PRIORITY)

    _expert_chunk_pairs(nc_ref[e], lambda c, s, h: in_copy_abs(c0 + c, s, h),
                        lambda c, s, h: out_copy_abs(c0 + c, s, h), compute, step == 0, prefetch_next)

    @pl.when(e == n_exp - 1)
    def _():
        _zero_tail(tail_ref, obuf, out_copy_abs)


def moe_down(act, w_down, b_down, layer, chunk_start, n_chunks, tail):
    nk, p, tk = act.shape
    n_exp, f, d = w_down.shape[1:]
    tn = min(1024, d)
    any_spec = pl.BlockSpec(memory_space=pl.ANY)
    grid_spec = pltpu.PrefetchScalarGridSpec(
        num_scalar_prefetch=3,
        grid=(d // tn, n_exp),
        in_specs=[any_spec,
                  pl.BlockSpec((None, None, f, tn), lambda j, e, *_: (layer, e, 0, j)),
                  pl.BlockSpec((None, None, 1, tn), lambda j, e, *_: (layer, e, 0, j))],
        out_specs=any_spec,
        scratch_shapes=[pltpu.VMEM((f, tn), BF16),
                        pltpu.VMEM((2, nk, 2 * MOE_ROWS, tk), BF16), pltpu.VMEM((2, 2 * MOE_ROWS, tn), F32),
                        pltpu.SemaphoreType.DMA((2,)), pltpu.SemaphoreType.DMA((2,))])
    return pl.pallas_call(
        _moe_down_kernel,
        grid_spec=grid_spec,
        out_shape=jax.ShapeDtypeStruct((p, d), F32),
        compiler_params=_params(("arbitrary", "arbitrary")),
        name="moe_down",
    )(chunk_start, n_chunks, tail, act, w_down, b_down)


def _combine_kernel(idx_ref, idx_next_ref, gate_ref, rows_ref, o_ref, buf_ref, sem, *, tc):
    i = pl.program_id(0)
    slot = lax.rem(i, 2)

    def copy(sl, k, r, src_row):
        return pltpu.make_async_copy(rows_ref.at[pl.ds(src_row, 1)], buf_ref.at[sl, k, pl.ds(r, 1)], sem.at[sl])

    def start_block(rows_idx_ref, sl):
        for k in range(TOP_K):
            for r in range(tc):
                copy(sl, k, r, rows_idx_ref[0, k * tc + r]).start(priority=r % 2)

    @pl.when(i == 0)
    def _():
        start_block(idx_ref, 0)

    @pl.when(i + 1 < pl.num_programs(0))
    def _():
        start_block(idx_next_ref, 1 - slot)

    for k in range(TOP_K):
        for r in range(tc):
            copy(slot, k, r, 0).wait()
    gate = gate_ref[...]
    acc = buf_ref[slot, 0] * gate[:, 0:1]
    for k in range(1, TOP_K):
        acc = acc + buf_ref[slot, k] * gate[:, k:k + 1]
    o_ref[...] = acc


def moe_combine(rows, dest, gates):
    n = dest.shape[0]
    d = rows.shape[1]
    tc = COMBINE_TOKENS
    assert n % tc == 0
    nblk = n // tc
    idx = dest.reshape(nblk, tc, TOP_K).transpose(0, 2, 1).reshape(nblk, 1, TOP_K * tc)
    return pl.pallas_call(
        functools.partial(_combine_kernel, tc=tc),
        grid=(nblk,),
        in_specs=[pl.BlockSpec((None, 1, TOP_K * tc), lambda i: (i, 0, 0), memory_space=pltpu.SMEM),
                  pl.BlockSpec((None, 1, TOP_K * tc), lambda i: (jnp.minimum(i + 1, nblk - 1), 0, 0),
                               memory_space=pltpu.SMEM),
                  pl.BlockSpec((tc, LANES), lambda i: (i, 0)),
                  pl.BlockSpec(memory_space=pl.ANY)],
        out_specs=pl.BlockSpec((tc, d), lambda i: (i, 0)),
        out_shape=jax.ShapeDtypeStruct((n, d), F32),
        scratch_shapes=[pltpu.VMEM((2, TOP_K, tc, d), F32), pltpu.SemaphoreType.DMA((2,))],
        compiler_params=_params(("arbitrary",)),
        name="moe_combine",
    )(idx, idx, gates, rows)


def moe_ffn(h_all, top_idx, top_gate, w_up, b_up, w_down, b_down, layer):
    n, d = h_all.shape
    n_exp = w_up.shape[1]
    n_pairs = n * TOP_K
    flat_e = top_idx[:, :TOP_K].reshape(-1)
    blk = 256
    assert n_pairs % blk == 0
    onehot = (flat_e[:, None] == jnp.arange(n_exp, dtype=I32)[None, :]).astype(F32).reshape(n_pairs // blk, blk, n_exp)
    within = jnp.einsum('ts,bse->bte', jnp.tril(jnp.ones((blk, blk), F32)), onehot)
    blk_tot = within[:, -1, :]
    blk_end = jnp.cumsum(blk_tot, axis=0)
    csum = within + (blk_end - blk_tot)[:, None, :]
    counts = blk_end[-1].astype(I32)
    rank = jnp.sum((csum - onehot) * onehot, axis=-1).reshape(-1).astype(I32)
    padded = (counts + MOE_ROWS - 1) // MOE_ROWS * MOE_ROWS
    pend = jnp.cumsum(padded)
    pstart = pend - padded
    dest = pstart[flat_e] + rank
    nb = -(-n_pairs // MOE_ROWS) + n_exp
    p = nb * MOE_ROWS
    row_tok = (jnp.arange(p, dtype=I32) % n).at[dest].set(jnp.arange(n_pairs, dtype=I32) // TOP_K,
                                                          unique_indices=True)
    chunk_start = (pstart // MOE_ROWS).astype(I32)
    n_chunks = (padded // MOE_ROWS).astype(I32)
    used = (pend[-1] // MOE_ROWS).astype(I32)
    tail = jnp.stack([used, nb - used])

    xs = gather_rows(h_all, row_tok, BF16)
    act = moe_up(xs, w_up, b_up, layer, chunk_start, n_chunks, tail)
    rows = moe_down(act, w_down, b_down, layer, chunk_start, n_chunks, tail)
    return moe_combine(rows, dest.reshape(n, TOP_K).astype(I32), top_gate)


def _pad_rows_front(a, rows):
    pad = rows - a.shape[1]
    return jnp.pad(a, ((0, 0), (pad, 0), (0, 0)))


def _pad_lanes(a):
    return jnp.pad(a, [(0, 0)] * (a.ndim - 1) + [(0, LANES - a.shape[-1])])


def kernel(x_prompt, x_sample, c_prompt, c_sample, cache_pool, cache_k, cache_v, cache_conv, state_C, state_n, state_m, ab_w_in, ab_w_pool, ab_pool_scale, ab_w_out, ml_w_in, ml_conv_w, ml_conv_b, ml_b_i, ml_b_f, ml_w_out, ada_w, ada_b, ln_g, ln_b, router_w, router_b, moe_w_up, moe_b_up, moe_w_down, moe_b_down):
    bp, tp, d = x_prompt.shape
    bs, ts, _ = x_sample.shape
    depth = ada_w.shape[0]
    alpha = (2 * depth) ** 0.25
    past, sb_heads, sb_dim = cache_k.shape[2:]
    sb_w = sb_heads * sb_dim
    pw = cache_pool.shape[-1]
    ml_heads, dk, dv = state_C.shape[2:]
    qkw = 2 * ml_heads * dk
    vw = ml_heads * dv
    n_exp = router_w.shape[-1]
    mp, ms = bp * tp, bs * ts
    n_tok = mp + ms

    n_c = bp + bs
    c_rows = -(-n_c // 16) * 16
    c_all = jnp.pad(jnp.concatenate([c_prompt, c_sample], axis=0), ((0, c_rows - n_c), (0, 0)))
    mod = adaln(c_all, ada_w, ada_b).reshape(depth, c_rows, 6, d)

    def terms(layer, which):
        rows = slice(0, bp) if which == 0 else slice(bp, bp + bs)
        return [mod[layer, rows, i, :][:, None, :] for i in range(6)]

    router_wp = _pad_lanes(router_w)
    router_bp = _pad_lanes(router_b)[:, None, :]
    b_up4 = moe_b_up[:, :, None, :]
    b_down4 = moe_b_down[:, :, None, :]

    tq = min(512, tp // 2)
    tk_att = min(256, tq)
    tkn = LANES
    tri_n = max(tk_att, min(512, past), tkn)
    tri = (jnp.arange(tri_n)[:, None] > jnp.arange(tri_n)[None, :]).astype(BF16)

    xs_res = [x_prompt.reshape(mp, d), x_sample.reshape(ms, d)]
    trunk_rows = [tp, ts]
    trunk_b = [bp, bs]
    row_off = [0, mp]
    h_mix = [None, None]
    pools, ks, vs, convs, cs, ns, mstates = [[[], []] for _ in range(7)]

    for layer in range(depth):
        j = layer // 2
        t6 = [terms(layer, 0), terms(layer, 1)]
        if layer == 0:
            for w in range(2):
                sh_m, sc_m = t6[w][0], t6[w][1]
                h_mix[w] = modulate(xs_res[w].reshape(trunk_b[w], trunk_rows[w], d), sc_m, sh_m)
        y_mix = [None, None]
        if layer % 2 == 0:
            w_in = ab_w_in[j].astype(BF16)
            w_out = ab_w_out[j].astype(BF16)
            w_pool = ab_w_pool[j].astype(BF16)
            scale = ab_pool_scale[j][None, :]
            for w in range(2):
                bb, tt = trunk_b[w], trunk_rows[w]
                u, q, k, v = [matmul([h_mix[w]], w_in, [0], c0, wd)
                              for c0, wd in ((0, pw), (pw, sb_w), (pw + sb_w, sb_w), (pw + 2 * sb_w, sb_w))]
                u3 = u.reshape(bb, tt, pw)
                if w == 0:
                    hist = jnp.zeros((bb, POOL_HALO, pw), F32)
                    pool_out = pool_mix(u3, hist, w_pool, scale, 0)
                    att = attn_prompt(q.reshape(bb, tt, sb_w), k.reshape(bb, tt, sb_w), v.reshape(bb, tt, sb_w),
                                      tri[:tk_att, :tk_att], sb_heads, tq)
                    pools[w].append(u3[:, tt - (POOL_HALO - 1):])
                else:
                    hist = _pad_rows_front(cache_pool[j], POOL_HALO)
                    pool_out = pool_mix(u3, hist, w_pool, scale, past)
                    z_pool = jnp.concatenate([cache_pool[j], u3], axis=1)
                    pools[w].append(z_pool[:, -(POOL_HALO - 1):])
                    q4 = q.reshape(bb, tt, sb_heads, sb_dim)
                    eye = jnp.eye(sb_heads, dtype=F32)
                    q_bd = jnp.einsum('bihd,hg->bhigd', q4, eye).reshape(bb, sb_heads * tt, sb_w).astype(BF16)
                    k_new = jnp.pad(k.reshape(bb, tt, sb_w), ((0, 0), (0, tkn - tt), (0, 0)))
                    v_new = jnp.pad(v.reshape(bb, tt, sb_w), ((0, 0), (0, tkn - tt), (0, 0)))
                    o_bd = attn_sample(q_bd, k_new, v_new, cache_k[j].reshape(bb, past * sb_heads, sb_dim),
                                       cache_v[j].reshape(bb, past * sb_heads, sb_dim), tri, tt, sb_dim)
                    o5 = o_bd.reshape(bb, sb_heads, tt, sb_heads, sb_dim)
                    att = jnp.einsum('bhihd->bihd', o5).reshape(bb * tt, sb_w).astype(BF16)
                ks[w].append(k.reshape(bb, tt, sb_heads, sb_dim))
                vs[w].append(v.reshape(bb, tt, sb_heads, sb_dim))
                y_mix[w] = matmul([pool_out, att], w_out, [0, pw], 0, d)
        else:
            w_in = ml_w_in[j]
            o1 = qkw
            o2 = o1 + vw
            o3 = o2 + vw
            w_main = w_in[:, :o3].astype(BF16)
            w_gate = _pad_lanes(w_in[:, o3:]).astype(BF16)
            w_out = ml_w_out[j].astype(BF16)
            conv_w = jnp.pad(ml_conv_w[j], ((0, CONV_HALO - CONV_W), (0, 0)))
            conv_b = ml_conv_b[j][None, :]
            gate_b = _pad_lanes(jnp.concatenate([ml_b_i[j], ml_b_f[j]]))[None, :]
            for w in range(2):
                bb, tt = trunk_b[w], trunk_rows[w]
                qk_pre = matmul([h_mix[w]], w_main, [0], 0, qkw).reshape(bb, tt, qkw)
                v_in = matmul([h_mix[w]], w_main, [0], o1, vw).reshape(bb, tt, vw)
                o_pre = matmul([h_mix[w]], w_main, [0], o2, vw).reshape(bb, tt, vw)
                gates = matmul([h_mix[w]], w_gate, [0], 0, LANES, tn=LANES).reshape(bb, tt, LANES)
                if w == 0:
                    hist = jnp.zeros((bb, CONV_HALO, qkw), F32)
                    c0 = jnp.zeros((bb, ml_heads, dk, dv), F32)
                    n0 = jnp.zeros((bb, ml_heads, dk), F32)
                    m0 = jnp.zeros((bb, 1, LANES), F32)
                    chunk = min(256, tt)
                    hid, c1, n1, m1 = mlstm_mix(qk_pre, v_in, o_pre, gates, conv_w, conv_b, gate_b, hist,
                                                c0, n0, m0, chunk, chunk)
                    convs[w].append(qk_pre[:, tt - (CONV_W - 1):])
                else:
                    hist = _pad_rows_front(cache_conv[j], CONV_HALO)
                    m0 = _pad_lanes(state_m[j])[:, None, :]
                    chunk = -(-tt // LANES) * LANES
                    padt = lambda a: jnp.pad(a, ((0, 0), (0, chunk - tt), (0, 0)))
                    hid, c1, n1, m1 = mlstm_mix(padt(qk_pre), padt(v_in), padt(o_pre), padt(gates), conv_w, conv_b,
                                                gate_b, hist, state_C[j], state_n[j], m0, chunk, tt)
                    hid = hid[:, :tt]
                    z_conv = jnp.concatenate([cache_conv[j], qk_pre], axis=1)
                    convs[w].append(z_conv[:, -(CONV_W - 1):])
                cs[w].append(c1)
                ns[w].append(n1)
                mstates[w].append(m1[:, 0, :ml_heads])
                y_mix[w] = matmul([hid.reshape(bb * tt, vw)], w_out, [0], 0, d)

        lg1, lb1 = ln_g[layer, 0][None, :], ln_b[layer, 0][None, :]
        lg2, lb2 = ln_g[layer, 1][None, :], ln_b[layer, 1][None, :]
        router = (router_wp[layer], router_bp[layer], n_exp)
        shared = (jnp.zeros((n_tok, d), F32), jnp.zeros((n_tok, LANES), I32), jnp.zeros((n_tok, LANES), F32))
        x_mid = [None, None]
        for w in range(2):
            g_m, sh_f, sc_f = t6[w][2], t6[w][3], t6[w][4]
            x_mid[w], *shared = deepnorm(xs_res[w], y_mix[w], 0, g_m, lg1, lb1, alpha, trunk_rows[w],
                                         nxt=(sc_f, sh_f), router=router, dest=tuple(shared),
                                         dest_row_off=row_off[w], next_dtype=F32)
        h_all, top_idx, top_gate = shared
        y_ffn = moe_ffn(h_all, top_idx, top_gate, moe_w_up, b_up4, moe_w_down, b_down4, layer)
        for w in range(2):
            g_f = t6[w][5]
            if layer + 1 < depth:
                t_next = terms(layer + 1, w)
                xs_res[w], h_mix[w] = deepnorm(x_mid[w], y_ffn, row_off[w], g_f, lg2, lb2, alpha, trunk_rows[w],
                                               nxt=(t_next[1], t_next[0]))
            else:
                (xs_res[w],) = deepnorm(x_mid[w], y_ffn, row_off[w], g_f, lg2, lb2, alpha, trunk_rows[w])

    outs = []
    for w in range(2):
        outs.append((xs_res[w].reshape(trunk_b[w], trunk_rows[w], d), jnp.stack(pools[w]), jnp.stack(ks[w]),
                     jnp.stack(vs[w]), jnp.stack(convs[w]), jnp.stack(cs[w]), jnp.stack(ns[w]),
                     jnp.stack(mstates[w])))
    (y_p, *rest_p), (y_s, *rest_s) = outs
    return (y_p, y_s, *rest_p, *rest_s)
```
